```python
import math
import jax, jax.numpy as jnp
from jax import lax
import numpy as np

D_MODEL = 1024
BATCH = 8
SEQ = 2048
DEPTH = 4
DEC_BATCH = 128
DEC_SEQ = 1
PAST_LEN = 16384
PAGE_SIZE = 128

CONV_WIDTH = D_MODEL // 2
CONV_HEADS = 8
CONV_K = 3
SSM_WIDTH = D_MODEL // 2
SSM_GROUP = 16
SSM_GROUPS = SSM_WIDTH // SSM_GROUP
SSM_STATE = 64
_FF_RAW = (8 * D_MODEL + 2) // 3
D_FF = ((_FF_RAW + 255) // 256) * 256
N_IN = 3 * CONV_WIDTH + SSM_WIDTH + 2 * D_MODEL
RMS_EPS = 1e-6
DT_MIN = 1e-3
DT_MAX = 1e-1

kernel_name = "gated_conv_s5_hybrid_step"


def _rmsnorm(x, g):
    xf = x.astype(jnp.float32)
    y = xf * lax.rsqrt(jnp.mean(xf * xf, axis=-1, keepdims=True) + RMS_EPS)
    return (y * g.astype(jnp.float32)).astype(x.dtype)


def _conv_causal(cin, buf, w, b):
    if buf is None:
        pad = jnp.pad(cin, ((0, 0), (CONV_K - 1, 0), (0, 0)))
    else:
        pad = jnp.concatenate([buf.astype(cin.dtype), cin], axis=1)
    L = cin.shape[1]
    out = b
    for k in range(CONV_K):
        out = out + w[k] * pad[:, k:k + L]
    return out, pad[:, pad.shape[1] - (CONV_K - 1):]


def _cmul(ar, ai, br, bi):
    return ar * br - ai * bi, ar * bi + ai * br


def _ssm_combine(e1, e2):
    a1r, a1i, b1r, b1i = e1
    a2r, a2i, b2r, b2i = e2
    ar, ai = _cmul(a2r, a2i, a1r, a1i)
    br, bi = _cmul(a2r, a2i, b1r, b1i)
    return ar, ai, br + b2r, bi + b2i


def _s5(u, h0_re, h0_im, lam_re, lam_im, log_dt, b_re, b_im, c_re, c_im, d_skip):
    N, L, _ = u.shape
    f32 = jnp.float32
    uf = u.astype(f32).reshape(N, L, SSM_GROUPS, SSM_GROUP)
    dt = jnp.exp(log_dt.astype(f32))[:, None]
    lr = lam_re.astype(f32)
    li = lam_im.astype(f32)
    mag = jnp.exp(lr * dt)
    ang = li * dt
    abar_r = mag * jnp.cos(ang)
    abar_i = mag * jnp.sin(ang)
    nr = abar_r - 1.0
    ni = abar_i
    den = lr * lr + li * li
    fr = ((nr * lr + ni * li) / den)[..., None]
    fi = ((ni * lr - nr * li) / den)[..., None]
    br = b_re.astype(f32)
    bi = b_im.astype(f32)
    bbar_r = fr * br - fi * bi
    bbar_i = fr * bi + fi * br
    bu_r = jnp.einsum('blgc,gpc->blgp', uf, bbar_r)
    bu_i = jnp.einsum('blgc,gpc->blgp', uf, bbar_i)
    a_r = jnp.broadcast_to(abar_r, bu_r.shape)
    a_i = jnp.broadcast_to(abar_i, bu_i.shape)
    A_r, A_i, h_r, h_i = lax.associative_scan(_ssm_combine, (a_r, a_i, bu_r, bu_i), axis=1)
    if h0_re is not None:
        pr, pi = _cmul(A_r, A_i, h0_re.astype(f32)[:, None], h0_im.astype(f32)[:, None])
        h_r = h_r + pr
        h_i = h_i + pi
    y = (jnp.einsum('blgp,gcp->blgc', h_r, c_re.astype(f32))
         - jnp.einsum('blgp,gcp->blgc', h_i, c_im.astype(f32)))
    y = y + d_skip.astype(f32).reshape(SSM_GROUPS, SSM_GROUP) * uf
    return (y.reshape(N, L, SSM_WIDTH).astype(u.dtype),
            h_r[:, L - 1].astype(u.dtype), h_i[:, L - 1].astype(u.dtype))


def _layer(x, conv_buf, h0_re, h0_im, norm1_g, w_in, conv_w, conv_b, lam_re, lam_im, log_dt,
           b_re, b_im, c_re, c_im, d_skip, w_glu, b_glu, w_branch, w_out, norm2_g, w_gate_up, w_down):
    h = _rmsnorm(x, norm1_g)
    z = h @ w_in
    cuts = [CONV_WIDTH, 2 * CONV_WIDTH, 3 * CONV_WIDTH, 3 * CONV_WIDTH + SSM_WIDTH,
            3 * CONV_WIDTH + SSM_WIDTH + D_MODEL]
    zb, zc, zv, zu, ga, gs = jnp.split(z, cuts, axis=-1)
    conv_out, new_buf = _conv_causal(zc * zv, conv_buf, conv_w, conv_b)
    ya = zb * conv_out
    ys, hr, hi = _s5(zu, h0_re, h0_im, lam_re, lam_im, log_dt, b_re, b_im, c_re, c_im, d_skip)
    gy = jax.nn.gelu(ys, approximate=False)
    ys = gy * jax.nn.sigmoid(gy @ w_glu + b_glu)
    oa = ya @ w_branch[:CONV_WIDTH]
    ob = ys @ w_branch[CONV_WIDTH:]
    m = jax.nn.sigmoid(ga) * oa + jax.nn.sigmoid(gs) * ob
    x = x + m @ w_out
    h2 = _rmsnorm(x, norm2_g)
    gate, up = jnp.split(h2 @ w_gate_up, [D_FF], axis=-1)
    x = x + (jax.nn.silu(gate) * up) @ w_down
    return x, new_buf, hr, hi


def _trunk(x, st_conv, st_re, st_im, norm1_g, w_in, conv_w, conv_b, ssm_lam_re, ssm_lam_im,
           ssm_log_dt, ssm_b_re, ssm_b_im, ssm_c_re, ssm_c_im, ssm_d, w_glu, b_glu, w_branch,
           w_out, norm2_g, w_gate_up, w_down, final_g):
    bufs, res, ims = [], [], []
    for l in range(DEPTH):
        cb = None if st_conv is None else st_conv[l]
        hr0 = None if st_re is None else st_re[l]
        hi0 = None if st_im is None else st_im[l]
        x, nb, nr, ni = _layer(x, cb, hr0, hi0, norm1_g[l], w_in[l], conv_w[l], conv_b[l],
                               ssm_lam_re[l], ssm_lam_im[l], ssm_log_dt[l], ssm_b_re[l], ssm_b_im[l],
                               ssm_c_re[l], ssm_c_im[l], ssm_d[l], w_glu[l], b_glu[l], w_branch[l],
                               w_out[l], norm2_g[l], w_gate_up[l], w_down[l])
        bufs.append(nb)
        res.append(nr)
        ims.append(ni)
    return _rmsnorm(x, final_g), jnp.stack(bufs), jnp.stack(res), jnp.stack(ims)


def setup_inputs(seed: int = 0) -> dict:
    key = jax.random.key(seed)
    ks = jax.random.split(key, 32)
    f32 = jnp.float32
    nrm = lambda k, s, sc: jax.random.normal(k, s, f32) * sc
    lam_im_base = jnp.pi * jnp.arange(SSM_STATE, dtype=f32)
    return {
        "x_prompt": nrm(ks[0], (BATCH, SEQ, D_MODEL), 1.0),
        "x_sample": nrm(ks[1], (DEC_BATCH, DEC_SEQ, D_MODEL), 1.0),
        "state_conv": nrm(ks[2], (DEPTH, DEC_BATCH, CONV_K - 1, CONV_WIDTH), 1.0),
        "state_ssm_re": nrm(ks[3], (DEPTH, DEC_BATCH, SSM_GROUPS, SSM_STATE), 0.5),
        "state_ssm_im": nrm(ks[4], (DEPTH, DEC_BATCH, SSM_GROUPS, SSM_STATE), 0.5),
        "norm1_g": 1.0 + nrm(ks[5], (DEPTH, D_MODEL), 0.02),
        "w_in": nrm(ks[6], (DEPTH, D_MODEL, N_IN), D_MODEL ** -0.5),
        "conv_w": nrm(ks[7], (DEPTH, CONV_K, CONV_WIDTH), CONV_K ** -0.5),
        "conv_b": nrm(ks[8], (DEPTH, CONV_WIDTH), 0.01),
        "ssm_lam_re": -0.5 + nrm(ks[9], (DEPTH, SSM_GROUPS, SSM_STATE), 0.01),
        "ssm_lam_im": lam_im_base + nrm(ks[10], (DEPTH, SSM_GROUPS, SSM_STATE), 0.01),
        "ssm_log_dt": jax.random.uniform(ks[11], (DEPTH, SSM_GROUPS), f32,
                                         math.log(DT_MIN), math.log(DT_MAX)),
        "ssm_b_re": nrm(ks[12], (DEPTH, SSM_GROUPS, SSM_STATE, SSM_GROUP), (2 * SSM_GROUP) ** -0.5),
        "ssm_b_im": nrm(ks[13], (DEPTH, SSM_GROUPS, SSM_STATE, SSM_GROUP), (2 * SSM_GROUP) ** -0.5),
        "ssm_c_re": nrm(ks[14], (DEPTH, SSM_GROUPS, SSM_GROUP, SSM_STATE), SSM_STATE ** -0.5),
        "ssm_c_im": nrm(ks[15], (DEPTH, SSM_GROUPS, SSM_GROUP, SSM_STATE), SSM_STATE ** -0.5),
        "ssm_d": nrm(ks[16], (DEPTH, SSM_WIDTH), 1.0),
        "w_glu": nrm(ks[17], (DEPTH, SSM_WIDTH, SSM_WIDTH), SSM_WIDTH ** -0.5),
        "b_glu": nrm(ks[18], (DEPTH, SSM_WIDTH), 0.01),
        "w_branch": nrm(ks[19], (DEPTH, CONV_WIDTH + SSM_WIDTH, D_MODEL), (D_MODEL // 2) ** -0.5),
        "w_out": nrm(ks[20], (DEPTH, D_MODEL, D_MODEL), (2 * DEPTH * D_MODEL) ** -0.5),
        "norm2_g": 1.0 + nrm(ks[21], (DEPTH, D_MODEL), 0.02),
        "w_gate_up": nrm(ks[22], (DEPTH, D_MODEL, 2 * D_FF), D_MODEL ** -0.5),
        "w_down": nrm(ks[23], (DEPTH, D_FF, D_MODEL), (2 * DEPTH * D_FF) ** -0.5),
        "final_g": 1.0 + nrm(ks[24], (D_MODEL,), 0.02),
    }


def reference(x_prompt, x_sample, state_conv, state_ssm_re, state_ssm_im, norm1_g, w_in, conv_w,
              conv_b, ssm_lam_re, ssm_lam_im, ssm_log_dt, ssm_b_re, ssm_b_im, ssm_c_re, ssm_c_im,
              ssm_d, w_glu, b_glu, w_branch, w_out, norm2_g, w_gate_up, w_down, final_g):
    y_prompt, prompt_conv, prompt_ssm_re, prompt_ssm_im = _trunk(
        x_prompt, None, None, None, norm1_g, w_in, conv_w, conv_b, ssm_lam_re, ssm_lam_im,
        ssm_log_dt, ssm_b_re, ssm_b_im, ssm_c_re, ssm_c_im, ssm_d, w_glu, b_glu, w_branch,
        w_out, norm2_g, w_gate_up, w_down, final_g)
    y_sample, sample_conv, sample_ssm_re, sample_ssm_im = _trunk(
        x_sample, state_conv, state_ssm_re, state_ssm_im, norm1_g, w_in, conv_w, conv_b,
        ssm_lam_re, ssm_lam_im, ssm_log_dt, ssm_b_re, ssm_b_im, ssm_c_re, ssm_c_im, ssm_d,
        w_glu, b_glu, w_branch, w_out, norm2_g, w_gate_up, w_down, final_g)
    return (y_prompt, y_sample, prompt_conv, prompt_ssm_re, prompt_ssm_im,
            sample_conv, sample_ssm_re, sample_ssm_im)
```

```python
import functools

import jax
import jax.numpy as jnp
from jax import lax
from jax.experimental import pallas as pl
from jax.experimental.pallas import tpu as pltpu

D_MODEL = 1024
DEPTH = 4
CONV_WIDTH = 512
CONV_K = 3
SSM_WIDTH = 512
SSM_GROUP = 16
SSM_GROUPS = 32
SSM_STATE = 64
GP = SSM_GROUPS * SSM_STATE
D_FF = 2816
N_IN = 3 * CONV_WIDTH + SSM_WIDTH + 2 * D_MODEL
RMS_EPS = 1e-6

SUBLANES = 8
MXU_DIM = 256
GROUPS_PER_CHUNK = MXU_DIM // SSM_GROUP
N_CHUNKS = SSM_GROUPS // GROUPS_PER_CHUNK
CHUNK_STATE = GROUPS_PER_CHUNK * SSM_STATE

ROW_BLOCK = 512
SCAN_LANES = 512
VMEM_LIMIT = 56 * 1024 * 1024

_BF16 = jnp.bfloat16
_F32 = jnp.float32


def _dot(a, b):
    return jnp.dot(a, b, preferred_element_type=_F32)


def _rmsnorm(x, g):
    ms = jnp.mean(x * x, axis=-1, keepdims=True)
    return x * lax.rsqrt(ms + RMS_EPS) * g


def _gelu_exact(x):
    return 0.5 * x * (1.0 + lax.erf(x * (2.0 ** -0.5)))


def _ssm_prep_kernel(lam_re_ref, lam_im_ref, logdt_ref, bt_re_ref, bt_im_ref,
                     ct_re_ref, ct_im_ref, abar_ref, wbu_ref, cmat_ref):
    lr = lam_re_ref[0]
    li = lam_im_ref[0]
    dt = jnp.exp(logdt_ref[0])
    mag = jnp.exp(lr * dt)
    ang = li * dt
    abar_r = mag * jnp.cos(ang)
    abar_i = mag * jnp.sin(ang)
    nr = abar_r - 1.0
    ni = abar_i
    den = lr * lr + li * li
    fr = (nr * lr + ni * li) / den
    fi = (ni * lr - nr * li) / den
    abar_ref[0, 0:1, :] = abar_r
    abar_ref[0, 1:2, :] = abar_i

    br = bt_re_ref[0]
    bi = bt_im_ref[0]
    bbar_r = fr * br - fi * bi
    bbar_i = fr * bi + fi * br
    cr = ct_re_ref[0]
    ci = -ct_im_ref[0]

    rows = lax.broadcasted_iota(jnp.int32, (MXU_DIM, CHUNK_STATE), 0)
    cols = lax.broadcasted_iota(jnp.int32, (MXU_DIM, CHUNK_STATE), 1)
    same_group = ((rows >> (SSM_GROUP.bit_length() - 1))
                  == (cols >> (SSM_STATE.bit_length() - 1)))

    def block_diag(src, k):
        piece = src[:, k * CHUNK_STATE:(k + 1) * CHUNK_STATE]
        tiled = jnp.concatenate([piece] * GROUPS_PER_CHUNK, axis=0)
        return jnp.where(same_group, tiled, 0.0)

    for k in range(N_CHUNKS):
        wbu_ref[0, k, :, 0:CHUNK_STATE] = block_diag(bbar_r, k).astype(_BF16)
        wbu_ref[0, k, :, CHUNK_STATE:2 * CHUNK_STATE] = block_diag(bbar_i, k).astype(_BF16)
        cmat_ref[0, k, 0:CHUNK_STATE, :] = block_diag(cr, k).T.astype(_BF16)
        cmat_ref[0, k, CHUNK_STATE:2 * CHUNK_STATE, :] = block_diag(ci, k).T.astype(_BF16)


def _ssm_prep(lam_re, lam_im, log_dt, b_re, b_im, c_re, c_im):
    lam_re = lam_re.reshape(DEPTH, 1, GP)
    lam_im = lam_im.reshape(DEPTH, 1, GP)
    logdt = jnp.repeat(log_dt, SSM_STATE, axis=-1).reshape(DEPTH, 1, GP)
    bt_re = b_re.transpose(0, 3, 1, 2).reshape(DEPTH, SSM_GROUP, GP)
    bt_im = b_im.transpose(0, 3, 1, 2).reshape(DEPTH, SSM_GROUP, GP)
    ct_re = c_re.transpose(0, 2, 1, 3).reshape(DEPTH, SSM_GROUP, GP)
    ct_im = c_im.transpose(0, 2, 1, 3).reshape(DEPTH, SSM_GROUP, GP)
    vec = pl.BlockSpec((1, 1, GP), lambda l: (l, 0, 0))
    mat = pl.BlockSpec((1, SSM_GROUP, GP), lambda l: (l, 0, 0))
    return pl.pallas_call(
        _ssm_prep_kernel,
        grid=(DEPTH,),
        in_specs=[vec, vec, vec, mat, mat, mat, mat],
        out_specs=[
            pl.BlockSpec((1, 2, GP), lambda l: (l, 0, 0)),
            pl.BlockSpec((1, N_CHUNKS, MXU_DIM, 2 * CHUNK_STATE), lambda l: (l, 0, 0, 0)),
            pl.BlockSpec((1, N_CHUNKS, 2 * CHUNK_STATE, MXU_DIM), lambda l: (l, 0, 0, 0)),
        ],
        out_shape=[
            jax.ShapeDtypeStruct((DEPTH, 2, GP), _F32),
            jax.ShapeDtypeStruct((DEPTH, N_CHUNKS, MXU_DIM, 2 * CHUNK_STATE), _BF16),
            jax.ShapeDtypeStruct((DEPTH, N_CHUNKS, 2 * CHUNK_STATE, MXU_DIM), _BF16),
        ],
        name="ssm_prep",
    )(lam_re, lam_im, logdt, bt_re, bt_im, ct_re, ct_im)


def _mixer_body(x, g1_ref, win_ref, d_ref, wbu_ref, cmat_ref, wglu_ref, bglu_ref,
                wbr_ref, wout_ref, hsr_ref, hsi_ref, conv_fn, scan_fn):
    cw = CONV_WIDTH
    h = _rmsnorm(x, g1_ref[...]).astype(_BF16)
    zb = _dot(h, win_ref[:, 0:cw])
    zc = _dot(h, win_ref[:, cw:2 * cw])
    zv = _dot(h, win_ref[:, 2 * cw:3 * cw])
    ya = (zb * conv_fn(zc * zv)).astype(_BF16)

    zu = _dot(h, win_ref[:, 3 * cw:3 * cw + SSM_WIDTH])
    zub = zu.astype(_BF16)
    for k in range(N_CHUNKS):
        bu = _dot(zub[:, k * MXU_DIM:(k + 1) * MXU_DIM], wbu_ref[k])
        hsr_ref[:, k * CHUNK_STATE:(k + 1) * CHUNK_STATE] = bu[:, 0:CHUNK_STATE]
        hsi_ref[:, k * CHUNK_STATE:(k + 1) * CHUNK_STATE] = bu[:, CHUNK_STATE:2 * CHUNK_STATE]
    scan_fn()
    ys = []
    for k in range(N_CHUNKS):
        hr = hsr_ref[:, k * CHUNK_STATE:(k + 1) * CHUNK_STATE].astype(_BF16)
        hi = hsi_ref[:, k * CHUNK_STATE:(k + 1) * CHUNK_STATE].astype(_BF16)
        ys.append(_dot(hr, cmat_ref[k, 0:CHUNK_STATE, :])
                  + _dot(hi, cmat_ref[k, CHUNK_STATE:2 * CHUNK_STATE, :]))
    y = jnp.concatenate(ys, axis=1) + d_ref[...] * zu
    gy = _gelu_exact(y)
    ys = gy * jax.nn.sigmoid(_dot(gy.astype(_BF16), wglu_ref[...]) + bglu_ref[...])

    oa = _dot(ya, wbr_ref[0:cw, :])
    ob = _dot(ys.astype(_BF16), wbr_ref[cw:cw + SSM_WIDTH, :])
    g0 = 3 * cw + SSM_WIDTH
    ga = _dot(h, win_ref[:, g0:g0 + D_MODEL])
    gs = _dot(h, win_ref[:, g0 + D_MODEL:g0 + 2 * D_MODEL])
    m = jax.nn.sigmoid(ga) * oa + jax.nn.sigmoid(gs) * ob
    return x + _dot(m.astype(_BF16), wout_ref[...])


def _mixer_prompt_kernel(x_ref, g1_ref, win_ref, cw_ref, cb_ref, abar_ref, d_ref, wbu_ref,
                         cmat_ref, wglu_ref, bglu_ref, wbr_ref, wout_ref,
                         xo_ref, tail_ref, str_ref, sti_ref, hsr_ref, hsi_ref):
    rows = x_ref.shape[0]
    steps = rows // SUBLANES
    halo = (CONV_K - 1) * SUBLANES

    @pl.when(pl.program_id(0) == 0)
    def _():
        tail_ref[...] = jnp.zeros_like(tail_ref)
        str_ref[...] = jnp.zeros_like(str_ref)
        sti_ref[...] = jnp.zeros_like(sti_ref)

    def conv_fn(cin):
        pad = jnp.concatenate([tail_ref[...], cin], axis=0)
        out = cb_ref[...]
        for k in range(CONV_K):
            out = out + cw_ref[k:k + 1, :] * pad[k * SUBLANES:k * SUBLANES + rows]
        tail_ref[...] = cin[rows - halo:rows]
        return out

    def scan_fn():
        for c in range(GP // SCAN_LANES):
            sl = pl.ds(c * SCAN_LANES, SCAN_LANES)
            ar = jnp.broadcast_to(abar_ref[0:1, sl], (SUBLANES, SCAN_LANES))
            ai = jnp.broadcast_to(abar_ref[1:2, sl], (SUBLANES, SCAN_LANES))

            def step(t, carry):
                hr, hi = carry
                r0 = pl.multiple_of(t * SUBLANES, SUBLANES)
                nhr = ar * hr - ai * hi + hsr_ref[pl.ds(r0, SUBLANES), sl]
                nhi = ar * hi + ai * hr + hsi_ref[pl.ds(r0, SUBLANES), sl]
                hsr_ref[pl.ds(r0, SUBLANES), sl] = nhr
                hsi_ref[pl.ds(r0, SUBLANES), sl] = nhi
                return nhr, nhi

            hr, hi = lax.fori_loop(0, steps, step, (str_ref[:, sl], sti_ref[:, sl]), unroll=2)
            str_ref[:, sl] = hr
            sti_ref[:, sl] = hi

    xo_ref[...] = _mixer_body(x_ref[...], g1_ref, win_ref, d_ref, wbu_ref, cmat_ref, wglu_ref,
                              bglu_ref, wbr_ref, wout_ref, hsr_ref, hsi_ref, conv_fn, scan_fn)


def _mixer_sample_kernel(x_ref, buf_ref, h0r_ref, h0i_ref, g1_ref, win_ref, cw_ref, cb_ref,
                         abar_ref, d_ref, wbu_ref, cmat_ref, wglu_ref, bglu_ref, wbr_ref, wout_ref,
                         xo_ref, cin_ref, str_ref, sti_ref, hsr_ref, hsi_ref):
    def conv_fn(cin):
        cin_ref[...] = cin
        out = cb_ref[...]
        for k in range(CONV_K - 1):
            out = out + cw_ref[k:k + 1, :] * buf_ref[k]
        return out + cw_ref[CONV_K - 1:CONV_K, :] * cin

    def scan_fn():
        ar = abar_ref[0:1, :]
        ai = abar_ref[1:2, :]
        h0r = h0r_ref[...]
        h0i = h0i_ref[...]
        nhr = ar * h0r - ai * h0i + hsr_ref[...]
        nhi = ar * h0i + ai * h0r + hsi_ref[...]
        hsr_ref[...] = nhr
        hsi_ref[...] = nhi
        str_ref[...] = nhr
        sti_ref[...] = nhi

    xo_ref[...] = _mixer_body(x_ref[...], g1_ref, win_ref, d_ref, wbu_ref, cmat_ref, wglu_ref,
                              bglu_ref, wbr_ref, wout_ref, hsr_ref, hsi_ref, conv_fn, scan_fn)


def _resident(shape):
    return pl.BlockSpec(shape, lambda i: (0,) * len(shape), pipeline_mode=pl.Buffered(1))


def _mixer_weight_specs():
    return [
        _resident((1, D_MODEL)),
        _resident((D_MODEL, N_IN)),
        _resident((CONV_K, CONV_WIDTH)),
        _resident((1, CONV_WIDTH)),
        _resident((2, GP)),
        _resident((1, SSM_WIDTH)),
        _resident((N_CHUNKS, MXU_DIM, 2 * CHUNK_STATE)),
        _resident((N_CHUNKS, 2 * CHUNK_STATE, MXU_DIM)),
        _resident((SSM_WIDTH, SSM_WIDTH)),
        _resident((1, SSM_WIDTH)),
        _resident((CONV_WIDTH + SSM_WIDTH, D_MODEL)),
        _resident((D_MODEL, D_MODEL)),
    ]


def _mixer_prompt(x, weights):
    rows = x.shape[0]
    halo = (CONV_K - 1) * SUBLANES
    row_spec = pl.BlockSpec((ROW_BLOCK, D_MODEL), lambda i: (i, 0))
    return pl.pallas_call(
        _mixer_prompt_kernel,
        grid=(rows // ROW_BLOCK,),
        in_specs=[row_spec] + _mixer_weight_specs(),
        out_specs=[
            row_spec,
            pl.BlockSpec((halo, CONV_WIDTH), lambda i: (0, 0)),
            pl.BlockSpec((SUBLANES, GP), lambda i: (0, 0)),
            pl.BlockSpec((SUBLANES, GP), lambda i: (0, 0)),
        ],
        out_shape=[
            jax.ShapeDtypeStruct((rows, D_MODEL), _F32),
            jax.ShapeDtypeStruct((halo, CONV_WIDTH), _F32),
            jax.ShapeDtypeStruct((SUBLANES, GP), _F32),
            jax.ShapeDtypeStruct((SUBLANES, GP), _F32),
        ],
        scratch_shapes=[pltpu.VMEM((ROW_BLOCK, GP), _F32), pltpu.VMEM((ROW_BLOCK, GP), _F32)],
        compiler_params=pltpu.CompilerParams(
            dimension_semantics=("arbitrary",), vmem_limit_bytes=VMEM_LIMIT),
        name="mixer_prompt",
    )(x, *weights)


def _mixer_sample(x, buf, h0r, h0i, weights):
    rows = x.shape[0]
    full = lambda shape: pl.BlockSpec(shape, lambda i: (0,) * len(shape))
    return pl.pallas_call(
        _mixer_sample_kernel,
        grid=(1,),
        in_specs=[full((rows, D_MODEL)), full((CONV_K - 1, rows, CONV_WIDTH)),
                  full((rows, GP)), full((rows, GP))] + _mixer_weight_specs(),
        out_specs=[full((rows, D_MODEL)), full((rows, CONV_WIDTH)),
                   full((rows, GP)), full((rows, GP))],
        out_shape=[
            jax.ShapeDtypeStruct((rows, D_MODEL), _F32),
            jax.ShapeDtypeStruct((rows, CONV_WIDTH), _F32),
            jax.ShapeDtypeStruct((rows, GP), _F32),
            jax.ShapeDtypeStruct((rows, GP), _F32),
        ],
        scratch_shapes=[pltpu.VMEM((rows, GP), _F32), pltpu.VMEM((rows, GP), _F32)],
        compiler_params=pltpu.CompilerParams(
            dimension_semantics=("arbitrary",), vmem_limit_bytes=VMEM_LIMIT),
        name="mixer_sample",
    )(x, buf, h0r, h0i, *weights)


_FF_CHUNKS = ((0, 1024), (1024, 1024), (2048, 768))


def _ffn_kernel(final_norm, x_ref, g2_ref, wgu_ref, wdown_ref, gf_ref, xo_ref, act_ref):
    x = x_ref[...]
    h2 = _rmsnorm(x, g2_ref[...]).astype(_BF16)
    for c0, cn in _FF_CHUNKS:
        gate = _dot(h2, wgu_ref[:, c0:c0 + cn])
        up = _dot(h2, wgu_ref[:, D_FF + c0:D_FF + c0 + cn])
        act_ref[:, c0:c0 + cn] = (gate * jax.nn.sigmoid(gate) * up).astype(_BF16)
    xn = x + _dot(act_ref[...], wdown_ref[...])
    if final_norm:
        xn = _rmsnorm(xn, gf_ref[...])
    xo_ref[...] = xn


def _ffn(x, g2, wgu, wdown, gf, final_norm):
    rows = x.shape[0]
    rb = min(ROW_BLOCK, rows)
    row_spec = pl.BlockSpec((rb, D_MODEL), lambda i: (i, 0))
    return pl.pallas_call(
        functools.partial(_ffn_kernel, final_norm),
        grid=(rows // rb,),
        in_specs=[row_spec, _resident((1, D_MODEL)), _resident((D_MODEL, 2 * D_FF)),
                  _resident((D_FF, D_MODEL)), _resident((1, D_MODEL))],
        out_specs=row_spec,
        out_shape=jax.ShapeDtypeStruct((rows, D_MODEL), _F32),
        scratch_shapes=[pltpu.VMEM((rb, D_FF), _BF16)],
        compiler_params=pltpu.CompilerParams(
            dimension_semantics=("arbitrary",), vmem_limit_bytes=VMEM_LIMIT),
        name="ffn",
    )(x, g2, wgu, wdown, gf)


def kernel(x_prompt, x_sample, state_conv, state_ssm_re, state_ssm_im, norm1_g, w_in, conv_w,
           conv_b, ssm_lam_re, ssm_lam_im, ssm_log_dt, ssm_b_re, ssm_b_im, ssm_c_re, ssm_c_im,
           ssm_d, w_glu, b_glu, w_branch, w_out, norm2_g, w_gate_up, w_down, final_g):
    batch, seq, _ = x_prompt.shape
    dec_batch = x_sample.shape[0]
    assert batch == SUBLANES and x_sample.shape[1] == 1

    abar, wbu, cmat = _ssm_prep(ssm_lam_re, ssm_lam_im, ssm_log_dt, ssm_b_re, ssm_b_im,
                                ssm_c_re, ssm_c_im)
    w_in_b = w_in.astype(_BF16)
    w_glu_b = w_glu.astype(_BF16)
    w_branch_b = w_branch.astype(_BF16)
    w_out_b = w_out.astype(_BF16)
    w_gate_up_b = w_gate_up.astype(_BF16)
    w_down_b = w_down.astype(_BF16)
    final_g2 = final_g.reshape(1, D_MODEL)

    xp = x_prompt.transpose(1, 0, 2).reshape(seq * batch, D_MODEL)
    xs = x_sample.reshape(dec_batch, D_MODEL)
    bufs = state_conv.transpose(0, 2, 1, 3)
    h0r = state_ssm_re.reshape(DEPTH, dec_batch, GP)
    h0i = state_ssm_im.reshape(DEPTH, dec_batch, GP)

    p_conv, p_re, p_im, s_conv, s_re, s_im = [], [], [], [], [], []
    for l in range(DEPTH):
        weights = (norm1_g[l].reshape(1, D_MODEL), w_in_b[l], conv_w[l],
                   conv_b[l].reshape(1, CONV_WIDTH), abar[l], ssm_d[l].reshape(1, SSM_WIDTH),
                   wbu[l], cmat[l], w_glu_b[l], b_glu[l].reshape(1, SSM_WIDTH),
                   w_branch_b[l], w_out_b[l])
        g2 = norm2_g[l].reshape(1, D_MODEL)
        last = l == DEPTH - 1

        xp, tail, sr, si = _mixer_prompt(xp, weights)
        xp = _ffn(xp, g2, w_gate_up_b[l], w_down_b[l], final_g2, last)
        p_conv.append(tail.reshape(CONV_K - 1, batch, CONV_WIDTH).transpose(1, 0, 2))
        p_re.append(sr.reshape(batch, SSM_GROUPS, SSM_STATE))
        p_im.append(si.reshape(batch, SSM_GROUPS, SSM_STATE))

        xs, cin, ssr, ssi = _mixer_sample(xs, bufs[l], h0r[l], h0i[l], weights)
        xs = _ffn(xs, g2, w_gate_up_b[l], w_down_b[l], final_g2, last)
        s_conv.append(jnp.concatenate([state_conv[l][:, 1:], cin[:, None, :]], axis=1))
        s_re.append(ssr.reshape(dec_batch, SSM_GROUPS, SSM_STATE))
        s_im.append(ssi.reshape(dec_batch, SSM_GROUPS, SSM_STATE))

    y_prompt = xp.reshape(seq, batch, D_MODEL).transpose(1, 0, 2)
    y_sample = xs.reshape(dec_batch, 1, D_MODEL)
    return (y_prompt, y_sample, jnp.stack(p_conv), jnp.stack(p_re), jnp.stack(p_im),
            jnp.stack(s_conv), jnp.stack(s_re), jnp.stack(s_im))
```

```python
import functools

import jax
import jax.numpy as jnp
from jax import lax
from jax.experimental import pallas as pl
from jax.experimental.pallas import tpu as pltpu

D_MODEL = 1024
DEPTH = 4
CONV_WIDTH = 512
CONV_K = 3
SSM_WIDTH = 512
SSM_GROUP = 16
SSM_GROUPS = 32
SSM_STATE = 64
GP = SSM_GROUPS * SSM_STATE
D_FF = 2816
N_IN = 3 * CONV_WIDTH + SSM_WIDTH + 2 * D_MODEL
RMS_EPS = 1e-6

SUBLANES = 8
MXU_DIM = 256
GROUPS_PER_CHUNK = MXU_DIM // SSM_GROUP
N_CHUNKS = SSM_GROUPS // GROUPS_PER_CHUNK
CHUNK_STATE = GROUPS_PER_CHUNK * SSM_STATE

ROW_BLOCK = 512
FFN_ROW_BLOCK = 1024
SCAN_LANES = 512
VMEM_LIMIT = 56 * 1024 * 1024

_BF16 = jnp.bfloat16
_F32 = jnp.float32


def _dot(a, b):
    return jnp.dot(a, b, preferred_element_type=_F32)


def _rmsnorm(x, g):
    ms = jnp.mean(x * x, axis=-1, keepdims=True)
    return x * lax.rsqrt(ms + RMS_EPS) * g


def _gelu_exact(x):
    return 0.5 * x * (1.0 + lax.erf(x * (2.0 ** -0.5)))


def _ssm_prep_kernel(lam_re_ref, lam_im_ref, logdt_ref, bt_re_ref, bt_im_ref,
                     ct_re_ref, ct_im_ref, abar_ref, wbu_ref, cmat_ref):
    lr = lam_re_ref[0]
    li = lam_im_ref[0]
    dt = jnp.exp(logdt_ref[0])
    mag = jnp.exp(lr * dt)
    ang = li * dt
    abar_r = mag * jnp.cos(ang)
    abar_i = mag * jnp.sin(ang)
    nr = abar_r - 1.0
    ni = abar_i
    den = lr * lr + li * li
    fr = (nr * lr + ni * li) / den
    fi = (ni * lr - nr * li) / den
    abar_ref[0, 0:1, :] = abar_r
    abar_ref[0, 1:2, :] = abar_i

    br = bt_re_ref[0]
    bi = bt_im_ref[0]
    bbar_r = fr * br - fi * bi
    bbar_i = fr * bi + fi * br
    cr = ct_re_ref[0]
    ci = -ct_im_ref[0]

    rows = lax.broadcasted_iota(jnp.int32, (MXU_DIM, CHUNK_STATE), 0)
    cols = lax.broadcasted_iota(jnp.int32, (MXU_DIM, CHUNK_STATE), 1)
    same_group = ((rows >> (SSM_GROUP.bit_length() - 1))
                  == (cols >> (SSM_STATE.bit_length() - 1)))

    def block_diag(src, k):
        piece = src[:, k * CHUNK_STATE:(k + 1) * CHUNK_STATE]
        tiled = jnp.concatenate([piece] * GROUPS_PER_CHUNK, axis=0)
        return jnp.where(same_group, tiled, 0.0)

    for k in range(N_CHUNKS):
        wbu_ref[0, k, :, 0:CHUNK_STATE] = block_diag(bbar_r, k).astype(_BF16)
        wbu_ref[0, k, :, CHUNK_STATE:2 * CHUNK_STATE] = block_diag(bbar_i, k).astype(_BF16)
        cmat_ref[0, k, 0:CHUNK_STATE, :] = block_diag(cr, k).T.astype(_BF16)
        cmat_ref[0, k, CHUNK_STATE:2 * CHUNK_STATE, :] = block_diag(ci, k).T.astype(_BF16)


def _ssm_prep(lam_re, lam_im, log_dt, b_re, b_im, c_re, c_im):
    lam_re = lam_re.reshape(DEPTH, 1, GP)
    lam_im = lam_im.reshape(DEPTH, 1, GP)
    logdt = jnp.repeat(log_dt, SSM_STATE, axis=-1).reshape(DEPTH, 1, GP)
    bt_re = b_re.transpose(0, 3, 1, 2).reshape(DEPTH, SSM_GROUP, GP)
    bt_im = b_im.transpose(0, 3, 1, 2).reshape(DEPTH, SSM_GROUP, GP)
    ct_re = c_re.transpose(0, 2, 1, 3).reshape(DEPTH, SSM_GROUP, GP)
    ct_im = c_im.transpose(0, 2, 1, 3).reshape(DEPTH, SSM_GROUP, GP)
    vec = pl.BlockSpec((1, 1, GP), lambda l: (l, 0, 0))
    mat = pl.BlockSpec((1, SSM_GROUP, GP), lambda l: (l, 0, 0))
    return pl.pallas_call(
        _ssm_prep_kernel,
        grid=(DEPTH,),
        in_specs=[vec, vec, vec, mat, mat, mat, mat],
        out_specs=[
            pl.BlockSpec((1, 2, GP), lambda l: (l, 0, 0)),
            pl.BlockSpec((1, N_CHUNKS, MXU_DIM, 2 * CHUNK_STATE), lambda l: (l, 0, 0, 0)),
            pl.BlockSpec((1, N_CHUNKS, 2 * CHUNK_STATE, MXU_DIM), lambda l: (l, 0, 0, 0)),
        ],
        out_shape=[
            jax.ShapeDtypeStruct((DEPTH, 2, GP), _F32),
            jax.ShapeDtypeStruct((DEPTH, N_CHUNKS, MXU_DIM, 2 * CHUNK_STATE), _BF16),
            jax.ShapeDtypeStruct((DEPTH, N_CHUNKS, 2 * CHUNK_STATE, MXU_DIM), _BF16),
        ],
        name="ssm_prep",
    )(lam_re, lam_im, logdt, bt_re, bt_im, ct_re, ct_im)


def _mixer_body(x, g1_ref, win_ref, d_ref, wbu_ref, cmat_ref, wglu_ref, bglu_ref,
                wbr_ref, wout_ref, hsr_ref, hsi_ref, conv_fn, scan_fn):
    cw = CONV_WIDTH
    h = _rmsnorm(x, g1_ref[...]).astype(_BF16)
    zb = _dot(h, win_ref[:, 0:cw])
    zc = _dot(h, win_ref[:, cw:2 * cw])
    zv = _dot(h, win_ref[:, 2 * cw:3 * cw])
    ya = (zb * conv_fn(zc * zv)).astype(_BF16)

    zu = _dot(h, win_ref[:, 3 * cw:3 * cw + SSM_WIDTH])
    zub = zu.astype(_BF16)
    for k in range(N_CHUNKS):
        bu = _dot(zub[:, k * MXU_DIM:(k + 1) * MXU_DIM], wbu_ref[k])
        hsr_ref[:, k * CHUNK_STATE:(k + 1) * CHUNK_STATE] = bu[:, 0:CHUNK_STATE]
        hsi_ref[:, k * CHUNK_STATE:(k + 1) * CHUNK_STATE] = bu[:, CHUNK_STATE:2 * CHUNK_STATE]
    scan_fn()
    ys = []
    for k in range(N_CHUNKS):
        hr = hsr_ref[:, k * CHUNK_STATE:(k + 1) * CHUNK_STATE].astype(_BF16)
        hi = hsi_ref[:, k * CHUNK_STATE:(k + 1) * CHUNK_STATE].astype(_BF16)
        ys.append(_dot(hr, cmat_ref[k, 0:CHUNK_STATE, :])
                  + _dot(hi, cmat_ref[k, CHUNK_STATE:2 * CHUNK_STATE, :]))
    y = jnp.concatenate(ys, axis=1) + d_ref[...] * zu
    gy = _gelu_exact(y)
    ys = gy * jax.nn.sigmoid(_dot(gy.astype(_BF16), wglu_ref[...]) + bglu_ref[...])

    oa = _dot(ya, wbr_ref[0:cw, :])
    ob = _dot(ys.astype(_BF16), wbr_ref[cw:cw + SSM_WIDTH, :])
    g0 = 3 * cw + SSM_WIDTH
    ga = _dot(h, win_ref[:, g0:g0 + D_MODEL])
    gs = _dot(h, win_ref[:, g0 + D_MODEL:g0 + 2 * D_MODEL])
    m = jax.nn.sigmoid(ga) * oa + jax.nn.sigmoid(gs) * ob
    return x + _dot(m.astype(_BF16), wout_ref[...])


def _mixer_prompt_kernel(x_ref, g1_ref, win_ref, cw_ref, cb_ref, abar_ref, d_ref, wbu_ref,
                         cmat_ref, wglu_ref, bglu_ref, wbr_ref, wout_ref,
                         xo_ref, tail_ref, str_ref, sti_ref, hsr_ref, hsi_ref):
    rows = x_ref.shape[0]
    steps = rows // SUBLANES
    halo = (CONV_K - 1) * SUBLANES

    @pl.when(pl.program_id(0) == 0)
    def _():
        tail_ref[...] = jnp.zeros_like(tail_ref)
        str_ref[...] = jnp.zeros_like(str_ref)
        sti_ref[...] = jnp.zeros_like(sti_ref)

    def conv_fn(cin):
        pad = jnp.concatenate([tail_ref[...], cin], axis=0)
        out = cb_ref[...]
        for k in range(CONV_K):
            out = out + cw_ref[k:k + 1, :] * pad[k * SUBLANES:k * SUBLANES + rows]
        tail_ref[...] = cin[rows - halo:rows]
        return out

    def scan_fn():
        for c in range(GP // SCAN_LANES):
            sl = pl.ds(c * SCAN_LANES, SCAN_LANES)
            ar = jnp.broadcast_to(abar_ref[0:1, sl], (SUBLANES, SCAN_LANES))
            ai = jnp.broadcast_to(abar_ref[1:2, sl], (SUBLANES, SCAN_LANES))

            hr = str_ref[:, sl]
            hi = sti_ref[:, sl]
            for t in range(steps):
                rs = pl.ds(t * SUBLANES, SUBLANES)
                hr, hi = (ar * hr - ai * hi + hsr_ref[rs, sl],
                          ar * hi + ai * hr + hsi_ref[rs, sl])
                hsr_ref[rs, sl] = hr
                hsi_ref[rs, sl] = hi
            str_ref[:, sl] = hr
            sti_ref[:, sl] = hi

    xo_ref[...] = _mixer_body(x_ref[...], g1_ref, win_ref, d_ref, wbu_ref, cmat_ref, wglu_ref,
                              bglu_ref, wbr_ref, wout_ref, hsr_ref, hsi_ref, conv_fn, scan_fn)


def _mixer_sample_kernel(x_ref, buf_ref, h0r_ref, h0i_ref, g1_ref, win_ref, cw_ref, cb_ref,
                         abar_ref, d_ref, wbu_ref, cmat_ref, wglu_ref, bglu_ref, wbr_ref, wout_ref,
                         xo_ref, cin_ref, str_ref, sti_ref, hsr_ref, hsi_ref):
    def conv_fn(cin):
        cin_ref[...] = cin
        out = cb_ref[...]
        for k in range(CONV_K - 1):
            out = out + cw_ref[k:k + 1, :] * buf_ref[k]
        return out + cw_ref[CONV_K - 1:CONV_K, :] * cin

    def scan_fn():
        ar = abar_ref[0:1, :]
        ai = abar_ref[1:2, :]
        h0r = h0r_ref[...]
        h0i = h0i_ref[...]
        nhr = ar * h0r - ai * h0i + hsr_ref[...]
        nhi = ar * h0i + ai * h0r + hsi_ref[...]
        hsr_ref[...] = nhr
        hsi_ref[...] = nhi
        str_ref[...] = nhr
        sti_ref[...] = nhi

    xo_ref[...] = _mixer_body(x_ref[...], g1_ref, win_ref, d_ref, wbu_ref, cmat_ref, wglu_ref,
                              bglu_ref, wbr_ref, wout_ref, hsr_ref, hsi_ref, conv_fn, scan_fn)


def _resident(layer, shape):
    return pl.BlockSpec((None,) + shape, lambda i: (layer,) + (0,) * len(shape),
                        pipeline_mode=pl.Buffered(1))


def _mixer_weight_specs(layer):
    return [
        _resident(layer, (1, D_MODEL)),
        _resident(layer, (D_MODEL, N_IN)),
        _resident(layer, (CONV_K, CONV_WIDTH)),
        _resident(layer, (1, CONV_WIDTH)),
        _resident(layer, (2, GP)),
        _resident(layer, (1, SSM_WIDTH)),
        _resident(layer, (N_CHUNKS, MXU_DIM, 2 * CHUNK_STATE)),
        _resident(layer, (N_CHUNKS, 2 * CHUNK_STATE, MXU_DIM)),
        _resident(layer, (SSM_WIDTH, SSM_WIDTH)),
        _resident(layer, (1, SSM_WIDTH)),
        _resident(layer, (CONV_WIDTH + SSM_WIDTH, D_MODEL)),
        _resident(layer, (D_MODEL, D_MODEL)),
    ]


def _mixer_prompt(layer, x, weights):
    rows = x.shape[0]
    halo = (CONV_K - 1) * SUBLANES
    row_spec = pl.BlockSpec((ROW_BLOCK, D_MODEL), lambda i: (i, 0))
    return pl.pallas_call(
        _mixer_prompt_kernel,
        grid=(rows // ROW_BLOCK,),
        in_specs=[row_spec] + _mixer_weight_specs(layer),
        out_specs=[
            row_spec,
            pl.BlockSpec((halo, CONV_WIDTH), lambda i: (0, 0)),
            pl.BlockSpec((SUBLANES, GP), lambda i: (0, 0)),
            pl.BlockSpec((SUBLANES, GP), lambda i: (0, 0)),
        ],
        out_shape=[
            jax.ShapeDtypeStruct((rows, D_MODEL), _F32),
            jax.ShapeDtypeStruct((halo, CONV_WIDTH), _F32),
            jax.ShapeDtypeStruct((SUBLANES, GP), _F32),
            jax.ShapeDtypeStruct((SUBLANES, GP), _F32),
        ],
        scratch_shapes=[pltpu.VMEM((ROW_BLOCK, GP), _F32), pltpu.VMEM((ROW_BLOCK, GP), _F32)],
        compiler_params=pltpu.CompilerParams(
            dimension_semantics=("arbitrary",), vmem_limit_bytes=VMEM_LIMIT),
        name="mixer_prompt",
    )(x, *weights)


def _mixer_sample(layer, x, buf, h0r, h0i, weights):
    rows = x.shape[0]
    full = lambda shape: pl.BlockSpec(shape, lambda i: (0,) * len(shape))
    state = lambda shape: pl.BlockSpec((None,) + shape, lambda i: (layer,) + (0,) * len(shape))
    return pl.pallas_call(
        _mixer_sample_kernel,
        grid=(1,),
        in_specs=[full((rows, D_MODEL)), state((CONV_K - 1, rows, CONV_WIDTH)),
                  state((rows, GP)), state((rows, GP))] + _mixer_weight_specs(layer),
        out_specs=[full((rows, D_MODEL)), full((rows, CONV_WIDTH)),
                   full((rows, GP)), full((rows, GP))],
        out_shape=[
            jax.ShapeDtypeStruct((rows, D_MODEL), _F32),
            jax.ShapeDtypeStruct((rows, CONV_WIDTH), _F32),
            jax.ShapeDtypeStruct((rows, GP), _F32),
            jax.ShapeDtypeStruct((rows, GP), _F32),
        ],
        scratch_shapes=[pltpu.VMEM((rows, GP), _F32), pltpu.VMEM((rows, GP), _F32)],
        compiler_params=pltpu.CompilerParams(
            dimension_semantics=("arbitrary",), vmem_limit_bytes=VMEM_LIMIT),
        name="mixer_sample",
    )(x, buf, h0r, h0i, *weights)


_FF_CHUNKS = ((0, 1024), (1024, 1024), (2048, 768))


def _ffn_kernel(final_norm, x_ref, g2_ref, wgu_ref, wdown_ref, gf_ref, xo_ref, act_ref):
    x = x_ref[...]
    h2 = _rmsnorm(x, g2_ref[...]).astype(_BF16)
    for c0, cn in _FF_CHUNKS:
        gate = _dot(h2, wgu_ref[:, c0:c0 + cn])
        up = _dot(h2, wgu_ref[:, D_FF + c0:D_FF + c0 + cn])
        act_ref[:, c0:c0 + cn] = (gate * jax.nn.sigmoid(gate) * up).astype(_BF16)
    xn = x + _dot(act_ref[...], wdown_ref[...])
    if final_norm:
        xn = _rmsnorm(xn, gf_ref[...])
    xo_ref[...] = xn


def _ffn(layer, x, g2, wgu, wdown, gf, final_norm):
    rows = x.shape[0]
    rb = min(FFN_ROW_BLOCK, rows)
    row_spec = pl.BlockSpec((rb, D_MODEL), lambda i: (i, 0))
    return pl.pallas_call(
        functools.partial(_ffn_kernel, final_norm),
        grid=(rows // rb,),
        in_specs=[row_spec, _resident(layer, (1, D_MODEL)), _resident(layer, (D_MODEL, 2 * D_FF)),
                  _resident(layer, (D_FF, D_MODEL)), _resident(0, (1, D_MODEL))],
        out_specs=row_spec,
        out_shape=jax.ShapeDtypeStruct((rows, D_MODEL), _F32),
        scratch_shapes=[pltpu.VMEM((rb, D_FF), _BF16)],
        compiler_params=pltpu.CompilerParams(
            dimension_semantics=("arbitrary",), vmem_limit_bytes=VMEM_LIMIT),
        name="ffn",
    )(x, g2, wgu, wdown, gf)


def kernel(x_prompt, x_sample, state_conv, state_ssm_re, state_ssm_im, norm1_g, w_in, conv_w,
           conv_b, ssm_lam_re, ssm_lam_im, ssm_log_dt, ssm_b_re, ssm_b_im, ssm_c_re, ssm_c_im,
           ssm_d, w_glu, b_glu, w_branch, w_out, norm2_g, w_gate_up, w_down, final_g):
    batch, seq, _ = x_prompt.shape
    dec_batch = x_sample.shape[0]
    assert batch == SUBLANES and x_sample.shape[1] == 1

    abar, wbu, cmat = _ssm_prep(ssm_lam_re, ssm_lam_im, ssm_log_dt, ssm_b_re, ssm_b_im,
                                ssm_c_re, ssm_c_im)
    weights = (norm1_g.reshape(DEPTH, 1, D_MODEL), w_in.astype(_BF16), conv_w,
               conv_b.reshape(DEPTH, 1, CONV_WIDTH), abar, ssm_d.reshape(DEPTH, 1, SSM_WIDTH),
               wbu, cmat, w_glu.astype(_BF16), b_glu.reshape(DEPTH, 1, SSM_WIDTH),
               w_branch.astype(_BF16), w_out.astype(_BF16))
    g2 = norm2_g.reshape(DEPTH, 1, D_MODEL)
    w_gate_up_b = w_gate_up.astype(_BF16)
    w_down_b = w_down.astype(_BF16)
    final_g2 = final_g.reshape(1, 1, D_MODEL)

    xp = x_prompt.transpose(1, 0, 2).reshape(seq * batch, D_MODEL)
    xs = x_sample.reshape(dec_batch, D_MODEL)
    bufs = state_conv.transpose(0, 2, 1, 3)
    h0r = state_ssm_re.reshape(DEPTH, dec_batch, GP)
    h0i = state_ssm_im.reshape(DEPTH, dec_batch, GP)

    p_conv, p_re, p_im, s_conv, s_re, s_im = [], [], [], [], [], []
    for l in range(DEPTH):
        last = l == DEPTH - 1

        xp, tail, sr, si = _mixer_prompt(l, xp, weights)
        xp = _ffn(l, xp, g2, w_gate_up_b, w_down_b, final_g2, last)
        p_conv.append(tail.reshape(CONV_K - 1, batch, CONV_WIDTH).transpose(1, 0, 2))
        p_re.append(sr.reshape(batch, SSM_GROUPS, SSM_STATE))
        p_im.append(si.reshape(batch, SSM_GROUPS, SSM_STATE))

        xs, cin, ssr, ssi = _mixer_sample(l, xs, bufs, h0r, h0i, weights)
        xs = _ffn(l, xs, g2, w_gate_up_b, w_down_b, final_g2, last)
        s_conv.append(jnp.concatenate([state_conv[l][:, 1:], cin[:, None, :]], axis=1))
        s_re.append(ssr.reshape(dec_batch, SSM_GROUPS, SSM_STATE))
        s_im.append(ssi.reshape(dec_batch, SSM_GROUPS, SSM_STATE))

    y_prompt = xp.reshape(seq, batch, D_MODEL).transpose(1, 0, 2)
    y_sample = xs.reshape(dec_batch, 1, D_MODEL)
    return (y_prompt, y_sample, jnp.stack(p_conv), jnp.stack(p_re), jnp.stack(p_im),
            jnp.stack(s_conv), jnp.stack(s_re), jnp.stack(s_im))
```

```python
import functools

import jax
import jax.numpy as jnp
from jax import lax
from jax.experimental import pallas as pl
from jax.experimental.pallas import tpu as pltpu

D_MODEL = 1024
DEPTH = 4
CONV_WIDTH = 512
CONV_K = 3
SSM_WIDTH = 512
SSM_GROUP = 16
SSM_GROUPS = 32
SSM_STATE = 64
GP = SSM_GROUPS * SSM_STATE
D_FF = 2816
N_IN = 3 * CONV_WIDTH + SSM_WIDTH + 2 * D_MODEL
RMS_EPS = 1e-6

SUBLANES = 8
LANES = 128
MXU_DIM = 256
GROUPS_PER_CHUNK = MXU_DIM // SSM_GROUP
N_CHUNKS = SSM_GROUPS // GROUPS_PER_CHUNK
CHUNK_STATE = GROUPS_PER_CHUNK * SSM_STATE

ROW_BLOCK = 512
FFN_ROW_BLOCK = 1024
SCAN_LANES = 512
VMEM_LIMIT = 56 * 1024 * 1024

_BF16 = jnp.bfloat16
_F32 = jnp.float32


def _dot(a, b):
    return jnp.dot(a, b, preferred_element_type=_F32)


def _rmsnorm(x, g):
    ms = jnp.mean(x * x, axis=-1, keepdims=True)
    return x * lax.rsqrt(ms + RMS_EPS) * g


def _gelu_exact(x):
    return 0.5 * x * (1.0 + lax.erf(x * (2.0 ** -0.5)))


def _ssm_prep_kernel(lam_re_ref, lam_im_ref, logdt_ref, bt_re_ref, bt_im_ref,
                     ct_re_ref, ct_im_ref, abar_ref, wbu_ref, cmat_ref):
    lr = lam_re_ref[0]
    li = lam_im_ref[0]
    dt = jnp.exp(logdt_ref[0])
    mag = jnp.exp(lr * dt)
    ang = li * dt
    abar_r = mag * jnp.cos(ang)
    abar_i = mag * jnp.sin(ang)
    nr = abar_r - 1.0
    ni = abar_i
    den = lr * lr + li * li
    fr = (nr * lr + ni * li) / den
    fi = (ni * lr - nr * li) / den
    abar_ref[0, 0:1, :] = abar_r
    abar_ref[0, 1:2, :] = abar_i

    br = bt_re_ref[0]
    bi = bt_im_ref[0]
    bbar_r = fr * br - fi * bi
    bbar_i = fr * bi + fi * br
    cr = ct_re_ref[0]
    ci = -ct_im_ref[0]

    rows = lax.broadcasted_iota(jnp.int32, (MXU_DIM, CHUNK_STATE), 0)
    cols = lax.broadcasted_iota(jnp.int32, (MXU_DIM, CHUNK_STATE), 1)
    same_group = ((rows >> (SSM_GROUP.bit_length() - 1))
                  == (cols >> (SSM_STATE.bit_length() - 1)))

    def block_diag(src, k):
        piece = src[:, k * CHUNK_STATE:(k + 1) * CHUNK_STATE]
        tiled = jnp.concatenate([piece] * GROUPS_PER_CHUNK, axis=0)
        return jnp.where(same_group, tiled, 0.0)

    for k in range(N_CHUNKS):
        wbu_ref[0, k, :, 0:CHUNK_STATE] = block_diag(bbar_r, k).astype(_BF16)
        wbu_ref[0, k, :, CHUNK_STATE:2 * CHUNK_STATE] = block_diag(bbar_i, k).astype(_BF16)
        cmat_ref[0, k, 0:CHUNK_STATE, :] = block_diag(cr, k).T.astype(_BF16)
        cmat_ref[0, k, CHUNK_STATE:2 * CHUNK_STATE, :] = block_diag(ci, k).T.astype(_BF16)


def _ssm_prep(lam_re, lam_im, log_dt, b_re, b_im, c_re, c_im):
    lam_re = lam_re.reshape(DEPTH, 1, GP)
    lam_im = lam_im.reshape(DEPTH, 1, GP)
    logdt = jnp.repeat(log_dt, SSM_STATE, axis=-1).reshape(DEPTH, 1, GP)
    bt_re = b_re.transpose(0, 3, 1, 2).reshape(DEPTH, SSM_GROUP, GP)
    bt_im = b_im.transpose(0, 3, 1, 2).reshape(DEPTH, SSM_GROUP, GP)
    ct_re = c_re.transpose(0, 2, 1, 3).reshape(DEPTH, SSM_GROUP, GP)
    ct_im = c_im.transpose(0, 2, 1, 3).reshape(DEPTH, SSM_GROUP, GP)
    vec = pl.BlockSpec((1, 1, GP), lambda l: (l, 0, 0))
    mat = pl.BlockSpec((1, SSM_GROUP, GP), lambda l: (l, 0, 0))
    return pl.pallas_call(
        _ssm_prep_kernel,
        grid=(DEPTH,),
        in_specs=[vec, vec, vec, mat, mat, mat, mat],
        out_specs=[
            pl.BlockSpec((1, 2, GP), lambda l: (l, 0, 0)),
            pl.BlockSpec((1, N_CHUNKS, MXU_DIM, 2 * CHUNK_STATE), lambda l: (l, 0, 0, 0)),
            pl.BlockSpec((1, N_CHUNKS, 2 * CHUNK_STATE, MXU_DIM), lambda l: (l, 0, 0, 0)),
        ],
        out_shape=[
            jax.ShapeDtypeStruct((DEPTH, 2, GP), _F32),
            jax.ShapeDtypeStruct((DEPTH, N_CHUNKS, MXU_DIM, 2 * CHUNK_STATE), _BF16),
            jax.ShapeDtypeStruct((DEPTH, N_CHUNKS, 2 * CHUNK_STATE, MXU_DIM), _BF16),
        ],
        name="ssm_prep",
    )(lam_re, lam_im, logdt, bt_re, bt_im, ct_re, ct_im)


def _mixer_body(x, g1_ref, win_ref, d_ref, wbu_ref, cmat_ref, wglu_ref, bglu_ref,
                wbr_ref, wout_ref, hsr_ref, hsi_ref, conv_fn, scan_fn):
    cw = CONV_WIDTH
    h = _rmsnorm(x, g1_ref[...]).astype(_BF16)
    zb = _dot(h, win_ref[:, 0:cw])
    zc = _dot(h, win_ref[:, cw:2 * cw])
    zv = _dot(h, win_ref[:, 2 * cw:3 * cw])
    ya = (zb * conv_fn(zc * zv)).astype(_BF16)

    zu = _dot(h, win_ref[:, 3 * cw:3 * cw + SSM_WIDTH])
    zub = zu.astype(_BF16)
    for k in range(N_CHUNKS):
        bu = _dot(zub[:, k * MXU_DIM:(k + 1) * MXU_DIM], wbu_ref[k])
        hsr_ref[:, k * CHUNK_STATE:(k + 1) * CHUNK_STATE] = bu[:, 0:CHUNK_STATE]
        hsi_ref[:, k * CHUNK_STATE:(k + 1) * CHUNK_STATE] = bu[:, CHUNK_STATE:2 * CHUNK_STATE]
    scan_fn()
    ys = []
    for k in range(N_CHUNKS):
        hr = hsr_ref[:, k * CHUNK_STATE:(k + 1) * CHUNK_STATE].astype(_BF16)
        hi = hsi_ref[:, k * CHUNK_STATE:(k + 1) * CHUNK_STATE].astype(_BF16)
        ys.append(_dot(hr, cmat_ref[k, 0:CHUNK_STATE, :])
                  + _dot(hi, cmat_ref[k, CHUNK_STATE:2 * CHUNK_STATE, :]))
    y = jnp.concatenate(ys, axis=1) + d_ref[...] * zu
    gy = _gelu_exact(y)
    ys = gy * jax.nn.sigmoid(_dot(gy.astype(_BF16), wglu_ref[...]) + bglu_ref[...])

    oa = _dot(ya, wbr_ref[0:cw, :])
    ob = _dot(ys.astype(_BF16), wbr_ref[cw:cw + SSM_WIDTH, :])
    g0 = 3 * cw + SSM_WIDTH
    ga = _dot(h, win_ref[:, g0:g0 + D_MODEL])
    gs = _dot(h, win_ref[:, g0 + D_MODEL:g0 + 2 * D_MODEL])
    m = jax.nn.sigmoid(ga) * oa + jax.nn.sigmoid(gs) * ob
    return x + _dot(m.astype(_BF16), wout_ref[...])


def _to_time_major(x_ref, slab_ref):
    batch, steps, width = x_ref.shape
    for n in range(batch):
        for j in range(width // LANES):
            slab_ref[j, pl.ds(n, steps, stride=batch), :] = x_ref[n, :, j * LANES:(j + 1) * LANES]
    return jnp.concatenate([slab_ref[j] for j in range(width // LANES)], axis=1)


def _from_time_major(x, slab_ref, o_ref):
    batch, steps, width = o_ref.shape
    for j in range(width // LANES):
        slab_ref[j] = x[:, j * LANES:(j + 1) * LANES]
    for n in range(batch):
        for j in range(width // LANES):
            o_ref[n, :, j * LANES:(j + 1) * LANES] = slab_ref[j, pl.ds(n, steps, stride=batch), :]


def _mixer_prompt_kernel(native_in, x_ref, g1_ref, win_ref, cw_ref, cb_ref, abar_ref, d_ref,
                         wbu_ref, cmat_ref, wglu_ref, bglu_ref, wbr_ref, wout_ref,
                         xo_ref, tail_ref, str_ref, sti_ref, hsr_ref, hsi_ref, *slab_ref):
    rows = xo_ref.shape[0]
    steps = rows // SUBLANES
    halo = (CONV_K - 1) * SUBLANES

    @pl.when(pl.program_id(0) == 0)
    def _():
        tail_ref[...] = jnp.zeros_like(tail_ref)
        str_ref[...] = jnp.zeros_like(str_ref)
        sti_ref[...] = jnp.zeros_like(sti_ref)

    x = _to_time_major(x_ref, slab_ref[0]) if native_in else x_ref[...]

    def conv_fn(cin):
        pad = jnp.concatenate([tail_ref[...], cin], axis=0)
        out = cb_ref[...]
        for k in range(CONV_K):
            out = out + cw_ref[k:k + 1, :] * pad[k * SUBLANES:k * SUBLANES + rows]
        tail_ref[...] = cin[rows - halo:rows]
        return out

    def scan_fn():
        for c in range(GP // SCAN_LANES):
            sl = pl.ds(c * SCAN_LANES, SCAN_LANES)
            ar = jnp.broadcast_to(abar_ref[0:1, sl], (SUBLANES, SCAN_LANES))
            ai = jnp.broadcast_to(abar_ref[1:2, sl], (SUBLANES, SCAN_LANES))

            hr = str_ref[:, sl]
            hi = sti_ref[:, sl]
            for t in range(steps):
                rs = pl.ds(t * SUBLANES, SUBLANES)
                hr, hi = (ar * hr - ai * hi + hsr_ref[rs, sl],
                          ar * hi + ai * hr + hsi_ref[rs, sl])
                hsr_ref[rs, sl] = hr
                hsi_ref[rs, sl] = hi
            str_ref[:, sl] = hr
            sti_ref[:, sl] = hi

    xo_ref[...] = _mixer_body(x, g1_ref, win_ref, d_ref, wbu_ref, cmat_ref, wglu_ref,
                              bglu_ref, wbr_ref, wout_ref, hsr_ref, hsi_ref, conv_fn, scan_fn)


def _mixer_sample_kernel(x_ref, buf_ref, h0r_ref, h0i_ref, g1_ref, win_ref, cw_ref, cb_ref,
                         abar_ref, d_ref, wbu_ref, cmat_ref, wglu_ref, bglu_ref, wbr_ref, wout_ref,
                         xo_ref, cin_ref, str_ref, sti_ref, hsr_ref, hsi_ref):
    def conv_fn(cin):
        cin_ref[...] = cin
        out = cb_ref[...]
        for k in range(CONV_K - 1):
            out = out + cw_ref[k:k + 1, :] * buf_ref[k]
        return out + cw_ref[CONV_K - 1:CONV_K, :] * cin

    def scan_fn():
        ar = abar_ref[0:1, :]
        ai = abar_ref[1:2, :]
        h0r = h0r_ref[...]
        h0i = h0i_ref[...]
        nhr = ar * h0r - ai * h0i + hsr_ref[...]
        nhi = ar * h0i + ai * h0r + hsi_ref[...]
        hsr_ref[...] = nhr
        hsi_ref[...] = nhi
        str_ref[...] = nhr
        sti_ref[...] = nhi

    xo_ref[...] = _mixer_body(x_ref[...], g1_ref, win_ref, d_ref, wbu_ref, cmat_ref, wglu_ref,
                              bglu_ref, wbr_ref, wout_ref, hsr_ref, hsi_ref, conv_fn, scan_fn)


def _resident(layer, shape):
    return pl.BlockSpec((None,) + shape, lambda i: (layer,) + (0,) * len(shape),
                        pipeline_mode=pl.Buffered(1))


def _mixer_weight_specs(layer):
    return [
        _resident(layer, (1, D_MODEL)),
        _resident(layer, (D_MODEL, N_IN)),
        _resident(layer, (CONV_K, CONV_WIDTH)),
        _resident(layer, (1, CONV_WIDTH)),
        _resident(layer, (2, GP)),
        _resident(layer, (1, SSM_WIDTH)),
        _resident(layer, (N_CHUNKS, MXU_DIM, 2 * CHUNK_STATE)),
        _resident(layer, (N_CHUNKS, 2 * CHUNK_STATE, MXU_DIM)),
        _resident(layer, (SSM_WIDTH, SSM_WIDTH)),
        _resident(layer, (1, SSM_WIDTH)),
        _resident(layer, (CONV_WIDTH + SSM_WIDTH, D_MODEL)),
        _resident(layer, (D_MODEL, D_MODEL)),
    ]


def _slab_scratch(rows):
    return pltpu.VMEM((D_MODEL // LANES, rows, LANES), _F32)


def _mixer_prompt(layer, x, weights):
    native_in = x.ndim == 3
    rows = x.shape[0] * x.shape[1] if native_in else x.shape[0]
    halo = (CONV_K - 1) * SUBLANES
    row_spec = pl.BlockSpec((ROW_BLOCK, D_MODEL), lambda i: (i, 0))
    if native_in:
        in_spec = pl.BlockSpec((SUBLANES, ROW_BLOCK // SUBLANES, D_MODEL), lambda i: (0, i, 0))
    else:
        in_spec = row_spec
    return pl.pallas_call(
        functools.partial(_mixer_prompt_kernel, native_in),
        grid=(rows // ROW_BLOCK,),
        in_specs=[in_spec] + _mixer_weight_specs(layer),
        out_specs=[
            row_spec,
            pl.BlockSpec((halo, CONV_WIDTH), lambda i: (0, 0)),
            pl.BlockSpec((SUBLANES, GP), lambda i: (0, 0)),
            pl.BlockSpec((SUBLANES, GP), lambda i: (0, 0)),
        ],
        out_shape=[
            jax.ShapeDtypeStruct((rows, D_MODEL), _F32),
            jax.ShapeDtypeStruct((halo, CONV_WIDTH), _F32),
            jax.ShapeDtypeStruct((SUBLANES, GP), _F32),
            jax.ShapeDtypeStruct((SUBLANES, GP), _F32),
        ],
        scratch_shapes=([pltpu.VMEM((ROW_BLOCK, GP), _F32), pltpu.VMEM((ROW_BLOCK, GP), _F32)]
                        + ([_slab_scratch(ROW_BLOCK)] if native_in else [])),
        compiler_params=pltpu.CompilerParams(
            dimension_semantics=("arbitrary",), vmem_limit_bytes=VMEM_LIMIT),
        name="mixer_prompt",
    )(x, *weights)


def _mixer_sample(layer, x, buf, h0r, h0i, weights):
    rows = x.shape[0]
    full = lambda shape: pl.BlockSpec(shape, lambda i: (0,) * len(shape))
    state = lambda shape: pl.BlockSpec((None,) + shape, lambda i: (layer,) + (0,) * len(shape))
    return pl.pallas_call(
        _mixer_sample_kernel,
        grid=(1,),
        in_specs=[full((rows, D_MODEL)), state((CONV_K - 1, rows, CONV_WIDTH)),
                  state((rows, GP)), state((rows, GP))] + _mixer_weight_specs(layer),
        out_specs=[full((rows, D_MODEL)), full((rows, CONV_WIDTH)),
                   full((rows, GP)), full((rows, GP))],
        out_shape=[
            jax.ShapeDtypeStruct((rows, D_MODEL), _F32),
            jax.ShapeDtypeStruct((rows, CONV_WIDTH), _F32),
            jax.ShapeDtypeStruct((rows, GP), _F32),
            jax.ShapeDtypeStruct((rows, GP), _F32),
        ],
        scratch_shapes=[pltpu.VMEM((rows, GP), _F32), pltpu.VMEM((rows, GP), _F32)],
        compiler_params=pltpu.CompilerParams(
            dimension_semantics=("arbitrary",), vmem_limit_bytes=VMEM_LIMIT),
        name="mixer_sample",
    )(x, buf, h0r, h0i, *weights)


_FF_CHUNKS = ((0, 1024), (1024, 1024), (2048, 768))


def _ffn_kernel(final_norm, native_out, x_ref, g2_ref, wgu_ref, wdown_ref, gf_ref, xo_ref,
                act_ref, *slab_ref):
    x = x_ref[...]
    h2 = _rmsnorm(x, g2_ref[...]).astype(_BF16)
    for c0, cn in _FF_CHUNKS:
        gate = _dot(h2, wgu_ref[:, c0:c0 + cn])
        up = _dot(h2, wgu_ref[:, D_FF + c0:D_FF + c0 + cn])
        act_ref[:, c0:c0 + cn] = (gate * jax.nn.sigmoid(gate) * up).astype(_BF16)
    xn = x + _dot(act_ref[...], wdown_ref[...])
    if final_norm:
        xn = _rmsnorm(xn, gf_ref[...])
    if native_out:
        _from_time_major(xn, slab_ref[0], xo_ref)
    else:
        xo_ref[...] = xn


def _ffn(layer, x, g2, wgu, wdown, gf, final_norm, native_out):
    rows = x.shape[0]
    rb = min(FFN_ROW_BLOCK, rows)
    row_spec = pl.BlockSpec((rb, D_MODEL), lambda i: (i, 0))
    if native_out:
        out_spec = pl.BlockSpec((SUBLANES, rb // SUBLANES, D_MODEL), lambda i: (0, i, 0))
        out_shape = jax.ShapeDtypeStruct((SUBLANES, rows // SUBLANES, D_MODEL), _F32)
    else:
        out_spec = row_spec
        out_shape = jax.ShapeDtypeStruct((rows, D_MODEL), _F32)
    return pl.pallas_call(
        functools.partial(_ffn_kernel, final_norm, native_out),
        grid=(rows // rb,),
        in_specs=[row_spec, _resident(layer, (1, D_MODEL)), _resident(layer, (D_MODEL, 2 * D_FF)),
                  _resident(layer, (D_FF, D_MODEL)), _resident(0, (1, D_MODEL))],
        out_specs=out_spec,
        out_shape=out_shape,
        scratch_shapes=[pltpu.VMEM((rb, D_FF), _BF16)] + ([_slab_scratch(rb)] if native_out else []),
        compiler_params=pltpu.CompilerParams(
            dimension_semantics=("arbitrary",), vmem_limit_bytes=VMEM_LIMIT),
        name="ffn",
    )(x, g2, wgu, wdown, gf)


def kernel(x_prompt, x_sample, state_conv, state_ssm_re, state_ssm_im, norm1_g, w_in, conv_w,
           conv_b, ssm_lam_re, ssm_lam_im, ssm_log_dt, ssm_b_re, ssm_b_im, ssm_c_re, ssm_c_im,
           ssm_d, w_glu, b_glu, w_branch, w_out, norm2_g, w_gate_up, w_down, final_g):
    batch, seq, _ = x_prompt.shape
    dec_batch = x_sample.shape[0]
    assert batch == SUBLANES and x_sample.shape[1] == 1

    abar, wbu, cmat = _ssm_prep(ssm_lam_re, ssm_lam_im, ssm_log_dt, ssm_b_re, ssm_b_im,
                                ssm_c_re, ssm_c_im)
    weights = (norm1_g.reshape(DEPTH, 1, D_MODEL), w_in.astype(_BF16), conv_w,
               conv_b.reshape(DEPTH, 1, CONV_WIDTH), abar, ssm_d.reshape(DEPTH, 1, SSM_WIDTH),
               wbu, cmat, w_glu.astype(_BF16), b_glu.reshape(DEPTH, 1, SSM_WIDTH),
               w_branch.astype(_BF16), w_out.astype(_BF16))
    g2 = norm2_g.reshape(DEPTH, 1, D_MODEL)
    w_gate_up_b = w_gate_up.astype(_BF16)
    w_down_b = w_down.astype(_BF16)
    final_g2 = final_g.reshape(1, 1, D_MODEL)

    xp = x_prompt
    xs = x_sample.reshape(dec_batch, D_MODEL)
    bufs = state_conv.transpose(0, 2, 1, 3)
    h0r = state_ssm_re.reshape(DEPTH, dec_batch, GP)
    h0i = state_ssm_im.reshape(DEPTH, dec_batch, GP)

    p_conv, p_re, p_im, s_conv, s_re, s_im = [], [], [], [], [], []
    for l in range(DEPTH):
        last = l == DEPTH - 1

        xp, tail, sr, si = _mixer_prompt(l, xp, weights)
        xp = _ffn(l, xp, g2, w_gate_up_b, w_down_b, final_g2, last, native_out=last)
        p_conv.append(tail.reshape(CONV_K - 1, batch, CONV_WIDTH).transpose(1, 0, 2))
        p_re.append(sr.reshape(batch, SSM_GROUPS, SSM_STATE))
        p_im.append(si.reshape(batch, SSM_GROUPS, SSM_STATE))

        xs, cin, ssr, ssi = _mixer_sample(l, xs, bufs, h0r, h0i, weights)
        xs = _ffn(l, xs, g2, w_gate_up_b, w_down_b, final_g2, last, native_out=False)
        s_conv.append(jnp.concatenate([state_conv[l][:, 1:], cin[:, None, :]], axis=1))
        s_re.append(ssr.reshape(dec_batch, SSM_GROUPS, SSM_STATE))
        s_im.append(ssi.reshape(dec_batch, SSM_GROUPS, SSM_STATE))

    y_prompt = xp
    y_sample = xs.reshape(dec_batch, 1, D_MODEL)
    return (y_prompt, y_sample, jnp.stack(p_conv), jnp.stack(p_re), jnp.stack(p_im),
            jnp.stack(s_conv), jnp.stack(s_re), jnp.stack(s_im))
```

```python
import functools

import jax
import jax.numpy as jnp
from jax import lax
from jax.experimental import pallas as pl
from jax.experimental.pallas import tpu as pltpu

D_MODEL = 1024
DEPTH = 4
CONV_WIDTH = 512
CONV_K = 3
SSM_WIDTH = 512
SSM_GROUP = 16
SSM_GROUPS = 32
SSM_STATE = 64
GP = SSM_GROUPS * SSM_STATE
D_FF = 2816
N_IN = 3 * CONV_WIDTH + SSM_WIDTH + 2 * D_MODEL
RMS_EPS = 1e-6

SUBLANES = 8
LANES = 128
BF16_SUBLANES = 16
CAST_STEPS = 8
MXU_DIM = 256
GROUPS_PER_CHUNK = MXU_DIM // SSM_GROUP
N_CHUNKS = SSM_GROUPS // GROUPS_PER_CHUNK
CHUNK_STATE = GROUPS_PER_CHUNK * SSM_STATE

ROW_BLOCK = 512
FFN_ROW_BLOCK = 1024
SCAN_LANES = 512
VMEM_LIMIT = 56 * 1024 * 1024

_BF16 = jnp.bfloat16
_F32 = jnp.float32


def _dot(a, b):
    return jnp.dot(a, b, preferred_element_type=_F32)


def _rmsnorm(x, g):
    ms = jnp.mean(x * x, axis=-1, keepdims=True)
    return x * lax.rsqrt(ms + RMS_EPS) * g


def _gelu_exact(x):
    return 0.5 * x * (1.0 + lax.erf(x * (2.0 ** -0.5)))


def _ssm_prep_kernel(lam_re_ref, lam_im_ref, logdt_ref, bt_re_ref, bt_im_ref,
                     ct_re_ref, ct_im_ref, abar_ref, wbu_ref, cmat_ref):
    lr = lam_re_ref[0]
    li = lam_im_ref[0]
    dt = jnp.exp(logdt_ref[0])
    mag = jnp.exp(lr * dt)
    ang = li * dt
    abar_r = mag * jnp.cos(ang)
    abar_i = mag * jnp.sin(ang)
    nr = abar_r - 1.0
    ni = abar_i
    den = lr * lr + li * li
    fr = (nr * lr + ni * li) / den
    fi = (ni * lr - nr * li) / den
    abar_ref[0, 0:1, :] = abar_r
    abar_ref[0, 1:2, :] = abar_i

    br = bt_re_ref[0]
    bi = bt_im_ref[0]
    bbar_r = fr * br - fi * bi
    bbar_i = fr * bi + fi * br
    cr = ct_re_ref[0]
    ci = -ct_im_ref[0]

    rows = lax.broadcasted_iota(jnp.int32, (MXU_DIM, CHUNK_STATE), 0)
    cols = lax.broadcasted_iota(jnp.int32, (MXU_DIM, CHUNK_STATE), 1)
    same_group = ((rows >> (SSM_GROUP.bit_length() - 1))
                  == (cols >> (SSM_STATE.bit_length() - 1)))

    def block_diag(src, k):
        piece = src[:, k * CHUNK_STATE:(k + 1) * CHUNK_STATE]
        tiled = jnp.concatenate([piece] * GROUPS_PER_CHUNK, axis=0)
        return jnp.where(same_group, tiled, 0.0)

    for k in range(N_CHUNKS):
        wbu_ref[0, k, :, 0:CHUNK_STATE] = block_diag(bbar_r, k).astype(_BF16)
        wbu_ref[0, k, :, CHUNK_STATE:2 * CHUNK_STATE] = block_diag(bbar_i, k).astype(_BF16)
        cmat_ref[0, k, 0:CHUNK_STATE, :] = block_diag(cr, k).T.astype(_BF16)
        cmat_ref[0, k, CHUNK_STATE:2 * CHUNK_STATE, :] = block_diag(ci, k).T.astype(_BF16)


def _ssm_prep(lam_re, lam_im, log_dt, b_re, b_im, c_re, c_im):
    lam_re = lam_re.reshape(DEPTH, 1, GP)
    lam_im = lam_im.reshape(DEPTH, 1, GP)
    logdt = jnp.repeat(log_dt, SSM_STATE, axis=-1).reshape(DEPTH, 1, GP)
    bt_re = b_re.transpose(0, 3, 1, 2).reshape(DEPTH, SSM_GROUP, GP)
    bt_im = b_im.transpose(0, 3, 1, 2).reshape(DEPTH, SSM_GROUP, GP)
    ct_re = c_re.transpose(0, 2, 1, 3).reshape(DEPTH, SSM_GROUP, GP)
    ct_im = c_im.transpose(0, 2, 1, 3).reshape(DEPTH, SSM_GROUP, GP)
    vec = pl.BlockSpec((1, 1, GP), lambda l: (l, 0, 0))
    mat = pl.BlockSpec((1, SSM_GROUP, GP), lambda l: (l, 0, 0))
    return pl.pallas_call(
        _ssm_prep_kernel,
        grid=(DEPTH,),
        in_specs=[vec, vec, vec, mat, mat, mat, mat],
        out_specs=[
            pl.BlockSpec((1, 2, GP), lambda l: (l, 0, 0)),
            pl.BlockSpec((1, N_CHUNKS, MXU_DIM, 2 * CHUNK_STATE), lambda l: (l, 0, 0, 0)),
            pl.BlockSpec((1, N_CHUNKS, 2 * CHUNK_STATE, MXU_DIM), lambda l: (l, 0, 0, 0)),
        ],
        out_shape=[
            jax.ShapeDtypeStruct((DEPTH, 2, GP), _F32),
            jax.ShapeDtypeStruct((DEPTH, N_CHUNKS, MXU_DIM, 2 * CHUNK_STATE), _BF16),
            jax.ShapeDtypeStruct((DEPTH, N_CHUNKS, 2 * CHUNK_STATE, MXU_DIM), _BF16),
        ],
        name="ssm_prep",
    )(lam_re, lam_im, logdt, bt_re, bt_im, ct_re, ct_im)


def _mixer_body(x, g1_ref, win_ref, d_ref, wbu_ref, cmat_ref, wglu_ref, bglu_ref,
                wbr_ref, wout_ref, hsr_ref, hsi_ref, conv_fn, scan_fn):
    cw = CONV_WIDTH
    h = _rmsnorm(x, g1_ref[...]).astype(_BF16)
    zb = _dot(h, win_ref[:, 0:cw])
    zc = _dot(h, win_ref[:, cw:2 * cw])
    zv = _dot(h, win_ref[:, 2 * cw:3 * cw])
    ya = (zb * conv_fn(zc * zv)).astype(_BF16)

    zu = _dot(h, win_ref[:, 3 * cw:3 * cw + SSM_WIDTH])
    zub = zu.astype(_BF16)
    for k in range(N_CHUNKS):
        bu = _dot(zub[:, k * MXU_DIM:(k + 1) * MXU_DIM], wbu_ref[k])
        hsr_ref[:, k * CHUNK_STATE:(k + 1) * CHUNK_STATE] = bu[:, 0:CHUNK_STATE]
        hsi_ref[:, k * CHUNK_STATE:(k + 1) * CHUNK_STATE] = bu[:, CHUNK_STATE:2 * CHUNK_STATE]
    scan_fn()
    ys = []
    for k in range(N_CHUNKS):
        hr = hsr_ref[:, k * CHUNK_STATE:(k + 1) * CHUNK_STATE].astype(_BF16)
        hi = hsi_ref[:, k * CHUNK_STATE:(k + 1) * CHUNK_STATE].astype(_BF16)
        ys.append(_dot(hr, cmat_ref[k, 0:CHUNK_STATE, :])
                  + _dot(hi, cmat_ref[k, CHUNK_STATE:2 * CHUNK_STATE, :]))
    y = jnp.concatenate(ys, axis=1) + d_ref[...] * zu
    gy = _gelu_exact(y)
    ys = gy * jax.nn.sigmoid(_dot(gy.astype(_BF16), wglu_ref[...]) + bglu_ref[...])

    oa = _dot(ya, wbr_ref[0:cw, :])
    ob = _dot(ys.astype(_BF16), wbr_ref[cw:cw + SSM_WIDTH, :])
    g0 = 3 * cw + SSM_WIDTH
    ga = _dot(h, win_ref[:, g0:g0 + D_MODEL])
    gs = _dot(h, win_ref[:, g0 + D_MODEL:g0 + 2 * D_MODEL])
    m = jax.nn.sigmoid(ga) * oa + jax.nn.sigmoid(gs) * ob
    return x + _dot(m.astype(_BF16), wout_ref[...])


def _to_time_major(x_ref, slab_ref):
    batch, steps, width = x_ref.shape
    for n in range(batch):
        for j in range(width // LANES):
            slab_ref[j, pl.ds(n, steps, stride=batch), :] = x_ref[n, :, j * LANES:(j + 1) * LANES]
    return jnp.concatenate([slab_ref[j] for j in range(width // LANES)], axis=1)


def _from_time_major(x, slab_ref, o_ref):
    batch, steps, width = o_ref.shape
    for j in range(width // LANES):
        slab_ref[j] = x[:, j * LANES:(j + 1) * LANES]
    for n in range(batch):
        for j in range(width // LANES):
            o_ref[n, :, j * LANES:(j + 1) * LANES] = slab_ref[j, pl.ds(n, steps, stride=batch), :]


N_MIXER_WEIGHTS = 12


def _cast_chunks(src_refs, dst_refs):
    for src, dst in zip(src_refs, dst_refs):
        dst[...] = src[...].astype(_BF16)


def _mixer_prompt_kernel(native_in, n_cast, x_ref, *refs):
    (g1_ref, win_ref, cw_ref, cb_ref, abar_ref, d_ref, wbu_ref, cmat_ref, wglu_ref, bglu_ref,
     wbr_ref, wout_ref) = refs[:N_MIXER_WEIGHTS]
    refs = refs[N_MIXER_WEIGHTS:]
    cast_src, refs = refs[:n_cast], refs[n_cast:]
    xo_ref, tail_ref, str_ref, sti_ref = refs[:4]
    cast_dst, refs = refs[4:4 + n_cast], refs[4 + n_cast:]
    hsr_ref, hsi_ref, *slab_ref = refs

    rows = xo_ref.shape[0]
    steps = rows // SUBLANES
    halo = (CONV_K - 1) * SUBLANES
    _cast_chunks(cast_src, cast_dst)

    @pl.when(pl.program_id(0) == 0)
    def _():
        tail_ref[...] = jnp.zeros_like(tail_ref)
        str_ref[...] = jnp.zeros_like(str_ref)
        sti_ref[...] = jnp.zeros_like(sti_ref)

    x = _to_time_major(x_ref, slab_ref[0]) if native_in else x_ref[...]

    def conv_fn(cin):
        pad = jnp.concatenate([tail_ref[...], cin], axis=0)
        out = cb_ref[...]
        for k in range(CONV_K):
            out = out + cw_ref[k:k + 1, :] * pad[k * SUBLANES:k * SUBLANES + rows]
        tail_ref[...] = cin[rows - halo:rows]
        return out

    def scan_fn():
        for c in range(GP // SCAN_LANES):
            sl = pl.ds(c * SCAN_LANES, SCAN_LANES)
            ar = jnp.broadcast_to(abar_ref[0:1, sl], (SUBLANES, SCAN_LANES))
            ai = jnp.broadcast_to(abar_ref[1:2, sl], (SUBLANES, SCAN_LANES))

            hr = str_ref[:, sl]
            hi = sti_ref[:, sl]
            for t in range(steps):
                rs = pl.ds(t * SUBLANES, SUBLANES)
                hr, hi = (ar * hr - ai * hi + hsr_ref[rs, sl],
                          ar * hi + ai * hr + hsi_ref[rs, sl])
                hsr_ref[rs, sl] = hr
                hsi_ref[rs, sl] = hi
            str_ref[:, sl] = hr
            sti_ref[:, sl] = hi

    xo_ref[...] = _mixer_body(x, g1_ref, win_ref, d_ref, wbu_ref, cmat_ref, wglu_ref,
                              bglu_ref, wbr_ref, wout_ref, hsr_ref, hsi_ref, conv_fn, scan_fn)


def _mixer_sample_kernel(x_ref, buf_ref, h0r_ref, h0i_ref, g1_ref, win_ref, cw_ref, cb_ref,
                         abar_ref, d_ref, wbu_ref, cmat_ref, wglu_ref, bglu_ref, wbr_ref, wout_ref,
                         xo_ref, cin_ref, str_ref, sti_ref, hsr_ref, hsi_ref):
    def conv_fn(cin):
        cin_ref[...] = cin
        out = cb_ref[...]
        for k in range(CONV_K - 1):
            out = out + cw_ref[k:k + 1, :] * buf_ref[k]
        return out + cw_ref[CONV_K - 1:CONV_K, :] * cin

    def scan_fn():
        ar = abar_ref[0:1, :]
        ai = abar_ref[1:2, :]
        h0r = h0r_ref[...]
        h0i = h0i_ref[...]
        nhr = ar * h0r - ai * h0i + hsr_ref[...]
        nhi = ar * h0i + ai * h0r + hsi_ref[...]
        hsr_ref[...] = nhr
        hsi_ref[...] = nhi
        str_ref[...] = nhr
        sti_ref[...] = nhi

    xo_ref[...] = _mixer_body(x_ref[...], g1_ref, win_ref, d_ref, wbu_ref, cmat_ref, wglu_ref,
                              bglu_ref, wbr_ref, wout_ref, hsr_ref, hsi_ref, conv_fn, scan_fn)


def _resident(arr, layer):
    if arr.shape[0] == DEPTH and arr.ndim >= 3:
        shape = arr.shape[1:]
        return pl.BlockSpec((None,) + shape, lambda i: (layer,) + (0,) * len(shape),
                            pipeline_mode=pl.Buffered(1))
    return pl.BlockSpec(arr.shape, lambda i: (0,) * arr.ndim, pipeline_mode=pl.Buffered(1))


def _cast_specs(arr, layer, n_steps):
    _, rows, cols = arr.shape
    hold = 1
    while (rows * hold) % (n_steps * BF16_SUBLANES):
        hold *= 2
    chunk = rows * hold // n_steps
    in_spec = pl.BlockSpec((None, chunk, cols), lambda i: (layer, i // hold, 0))
    out_spec = pl.BlockSpec((chunk, cols), lambda i: (i // hold, 0))
    return in_spec, out_spec, jax.ShapeDtypeStruct((rows, cols), _BF16)


def _cast_kernel(n_cast, *refs):
    _cast_chunks(refs[:n_cast], refs[n_cast:])


def _cast_weights(arrs, layer, n_steps):
    specs = [_cast_specs(a, layer, n_steps) for a in arrs]
    return pl.pallas_call(
        functools.partial(_cast_kernel, len(arrs)),
        grid=(n_steps,),
        in_specs=[s[0] for s in specs],
        out_specs=[s[1] for s in specs],
        out_shape=[s[2] for s in specs],
        compiler_params=pltpu.CompilerParams(dimension_semantics=("arbitrary",)),
        name="cast_weights",
    )(*arrs)


def _slab_scratch(rows):
    return pltpu.VMEM((D_MODEL // LANES, rows, LANES), _F32)


def _mixer_prompt(layer, x, weights, cast):
    native_in = x.ndim == 3
    rows = x.shape[0] * x.shape[1] if native_in else x.shape[0]
    n_steps = rows // ROW_BLOCK
    halo = (CONV_K - 1) * SUBLANES
    row_spec = pl.BlockSpec((ROW_BLOCK, D_MODEL), lambda i: (i, 0))
    if native_in:
        in_spec = pl.BlockSpec((SUBLANES, ROW_BLOCK // SUBLANES, D_MODEL), lambda i: (0, i, 0))
    else:
        in_spec = row_spec
    cast_specs = [_cast_specs(a, layer, n_steps) for a in cast]
    return pl.pallas_call(
        functools.partial(_mixer_prompt_kernel, native_in, len(cast)),
        grid=(n_steps,),
        in_specs=([in_spec] + [_resident(w, layer) for w in weights]
                  + [s[0] for s in cast_specs]),
        out_specs=[
            row_spec,
            pl.BlockSpec((halo, CONV_WIDTH), lambda i: (0, 0)),
            pl.BlockSpec((SUBLANES, GP), lambda i: (0, 0)),
            pl.BlockSpec((SUBLANES, GP), lambda i: (0, 0)),
        ] + [s[1] for s in cast_specs],
        out_shape=[
            jax.ShapeDtypeStruct((rows, D_MODEL), _F32),
            jax.ShapeDtypeStruct((halo, CONV_WIDTH), _F32),
            jax.ShapeDtypeStruct((SUBLANES, GP), _F32),
            jax.ShapeDtypeStruct((SUBLANES, GP), _F32),
        ] + [s[2] for s in cast_specs],
        scratch_shapes=([pltpu.VMEM((ROW_BLOCK, GP), _F32), pltpu.VMEM((ROW_BLOCK, GP), _F32)]
                        + ([_slab_scratch(ROW_BLOCK)] if native_in else [])),
        compiler_params=pltpu.CompilerParams(
            dimension_semantics=("arbitrary",), vmem_limit_bytes=VMEM_LIMIT),
        name="mixer_prompt",
    )(x, *weights, *cast)


def _mixer_sample(layer, x, buf, h0r, h0i, weights):
    rows = x.shape[0]
    full = lambda shape: pl.BlockSpec(shape, lambda i: (0,) * len(shape))
    state = lambda shape: pl.BlockSpec((None,) + shape, lambda i: (layer,) + (0,) * len(shape))
    return pl.pallas_call(
        _mixer_sample_kernel,
        grid=(1,),
        in_specs=[full((rows, D_MODEL)), state((CONV_K - 1, rows, CONV_WIDTH)),
                  state((rows, GP)), state((rows, GP))] + [_resident(w, layer) for w in weights],
        out_specs=[full((rows, D_MODEL)), full((rows, CONV_WIDTH)),
                   full((rows, GP)), full((rows, GP))],
        out_shape=[
            jax.ShapeDtypeStruct((rows, D_MODEL), _F32),
            jax.ShapeDtypeStruct((rows, CONV_WIDTH), _F32),
            jax.ShapeDtypeStruct((rows, GP), _F32),
            jax.ShapeDtypeStruct((rows, GP), _F32),
        ],
        scratch_shapes=[pltpu.VMEM((rows, GP), _F32), pltpu.VMEM((rows, GP), _F32)],
        compiler_params=pltpu.CompilerParams(
            dimension_semantics=("arbitrary",), vmem_limit_bytes=VMEM_LIMIT),
        name="mixer_sample",
    )(x, buf, h0r, h0i, *weights)


_FF_CHUNKS = ((0, 1024), (1024, 1024), (2048, 768))


def _ffn_kernel(final_norm, native_out, n_cast, x_ref, g2_ref, wgu_ref, wdown_ref, gf_ref, *refs):
    cast_src, xo_ref = refs[:n_cast], refs[n_cast]
    cast_dst, (act_ref, *slab_ref) = refs[n_cast + 1:2 * n_cast + 1], refs[2 * n_cast + 1:]
    _cast_chunks(cast_src, cast_dst)
    x = x_ref[...]
    h2 = _rmsnorm(x, g2_ref[...]).astype(_BF16)
    for c0, cn in _FF_CHUNKS:
        gate = _dot(h2, wgu_ref[:, c0:c0 + cn])
        up = _dot(h2, wgu_ref[:, D_FF + c0:D_FF + c0 + cn])
        act_ref[:, c0:c0 + cn] = (gate * jax.nn.sigmoid(gate) * up).astype(_BF16)
    xn = x + _dot(act_ref[...], wdown_ref[...])
    if final_norm:
        xn = _rmsnorm(xn, gf_ref[...])
    if native_out:
        _from_time_major(xn, slab_ref[0], xo_ref)
    else:
        xo_ref[...] = xn


def _ffn(layer, x, g2, wgu, wdown, gf, final_norm, native_out, cast=()):
    rows = x.shape[0]
    rb = min(FFN_ROW_BLOCK, rows)
    cast_specs = [_cast_specs(a, layer + 1, rows // rb) for a in cast]
    row_spec = pl.BlockSpec((rb, D_MODEL), lambda i: (i, 0))
    if native_out:
        out_spec = pl.BlockSpec((SUBLANES, rb // SUBLANES, D_MODEL), lambda i: (0, i, 0))
        out_shape = jax.ShapeDtypeStruct((SUBLANES, rows // SUBLANES, D_MODEL), _F32)
    else:
        out_spec = row_spec
        out_shape = jax.ShapeDtypeStruct((rows, D_MODEL), _F32)
    outs = pl.pallas_call(
        functools.partial(_ffn_kernel, final_norm, native_out, len(cast)),
        grid=(rows // rb,),
        in_specs=([row_spec, _resident(g2, layer), _resident(wgu, layer), _resident(wdown, layer),
                   _resident(gf, 0)] + [s[0] for s in cast_specs]),
        out_specs=[out_spec] + [s[1] for s in cast_specs],
        out_shape=[out_shape] + [s[2] for s in cast_specs],
        scratch_shapes=[pltpu.VMEM((rb, D_FF), _BF16)] + ([_slab_scratch(rb)] if native_out else []),
        compiler_params=pltpu.CompilerParams(
            dimension_semantics=("arbitrary",), vmem_limit_bytes=VMEM_LIMIT),
        name="ffn",
    )(x, g2, wgu, wdown, gf, *cast)
    return outs if cast else outs[0]


def kernel(x_prompt, x_sample, state_conv, state_ssm_re, state_ssm_im, norm1_g, w_in, conv_w,
           conv_b, ssm_lam_re, ssm_lam_im, ssm_log_dt, ssm_b_re, ssm_b_im, ssm_c_re, ssm_c_im,
           ssm_d, w_glu, b_glu, w_branch, w_out, norm2_g, w_gate_up, w_down, final_g):
    batch, seq, _ = x_prompt.shape
    dec_batch = x_sample.shape[0]
    assert batch == SUBLANES and x_sample.shape[1] == 1

    abar, wbu, cmat = _ssm_prep(ssm_lam_re, ssm_lam_im, ssm_log_dt, ssm_b_re, ssm_b_im,
                                ssm_c_re, ssm_c_im)
    g1 = norm1_g.reshape(DEPTH, 1, D_MODEL)
    g2 = norm2_g.reshape(DEPTH, 1, D_MODEL)
    gf = final_g.reshape(1, D_MODEL)
    cb = conv_b.reshape(DEPTH, 1, CONV_WIDTH)
    d_skip = ssm_d.reshape(DEPTH, 1, SSM_WIDTH)
    bg = b_glu.reshape(DEPTH, 1, SSM_WIDTH)
    mixer_f32 = (w_in, w_glu, w_branch, w_out)
    ffn_f32 = (w_gate_up, w_down)
    mixer_b = _cast_weights(mixer_f32, 0, CAST_STEPS)

    xp = x_prompt
    xs = x_sample.reshape(dec_batch, D_MODEL)
    bufs = state_conv.transpose(0, 2, 1, 3)
    h0r = state_ssm_re.reshape(DEPTH, dec_batch, GP)
    h0i = state_ssm_im.reshape(DEPTH, dec_batch, GP)

    p_conv, p_re, p_im, s_conv, s_re, s_im = [], [], [], [], [], []
    for l in range(DEPTH):
        last = l == DEPTH - 1
        w_in_b, w_glu_b, w_branch_b, w_out_b = mixer_b
        weights = (g1, w_in_b, conv_w, cb, abar, d_skip, wbu, cmat, w_glu_b, bg,
                   w_branch_b, w_out_b)
        assert len(weights) == N_MIXER_WEIGHTS

        xp, tail, sr, si, w_gate_up_b, w_down_b = _mixer_prompt(l, xp, weights, ffn_f32)
        if last:
            xp = _ffn(l, xp, g2, w_gate_up_b, w_down_b, gf, True, native_out=True)
        else:
            xp, *mixer_b = _ffn(l, xp, g2, w_gate_up_b, w_down_b, gf, False, native_out=False,
                                cast=mixer_f32)
        p_conv.append(tail.reshape(CONV_K - 1, batch, CONV_WIDTH).transpose(1, 0, 2))
        p_re.append(sr.reshape(batch, SSM_GROUPS, SSM_STATE))
        p_im.append(si.reshape(batch, SSM_GROUPS, SSM_STATE))

        xs, cin, ssr, ssi = _mixer_sample(l, xs, bufs, h0r, h0i, weights)
        xs = _ffn(l, xs, g2, w_gate_up_b, w_down_b, gf, last, native_out=False)
        s_conv.append(jnp.concatenate([state_conv[l][:, 1:], cin[:, None, :]], axis=1))
        s_re.append(ssr.reshape(dec_batch, SSM_GROUPS, SSM_STATE))
        s_im.append(ssi.reshape(dec_batch, SSM_GROUPS, SSM_STATE))

    y_prompt = xp
    y_sample = xs.reshape(dec_batch, 1, D_MODEL)
    return (y_prompt, y_sample, jnp.stack(p_conv), jnp.stack(p_re), jnp.stack(p_im),
            jnp.stack(s_conv), jnp.stack(s_re), jnp.stack(s_im))
```

```python
import functools

import jax
import jax.numpy as jnp
from jax import lax
from jax.experimental import pallas as pl
from jax.experimental.pallas import tpu as pltpu

D_MODEL = 1024
DEPTH = 4
CONV_WIDTH = 512
CONV_K = 3
SSM_WIDTH = 512
SSM_GROUP = 16
SSM_GROUPS = 32
SSM_STATE = 64
GP = SSM_GROUPS * SSM_STATE
D_FF = 2816
N_IN = 3 * CONV_WIDTH + SSM_WIDTH + 2 * D_MODEL
RMS_EPS = 1e-6

SUBLANES = 8
LANES = 128
BF16_SUBLANES = 16
CAST_STEPS = 8
MXU_DIM = 256
GROUPS_PER_CHUNK = MXU_DIM // SSM_GROUP
N_CHUNKS = SSM_GROUPS // GROUPS_PER_CHUNK
CHUNK_STATE = GROUPS_PER_CHUNK * SSM_STATE

CHUNK_T = MXU_DIM // SSM_GROUP
CHUNK_ROWS = CHUNK_T * SUBLANES
STATE2 = 2 * SSM_STATE
GROUP_LANES = SSM_GROUPS * MXU_DIM
S5_GROUPS_PER_STEP = 4

ROW_BLOCK = 512
FFN_ROW_BLOCK = 1024
VMEM_LIMIT = 56 * 1024 * 1024

_BF16 = jnp.bfloat16
_F32 = jnp.float32


def _dot(a, b):
    return jnp.dot(a, b, preferred_element_type=_F32)


def _dot_nt(a, b):
    return lax.dot_general(a, b, (((1,), (1,)), ((), ())), preferred_element_type=_F32)


def _rmsnorm(x, g):
    ms = jnp.mean(x * x, axis=-1, keepdims=True)
    return x * lax.rsqrt(ms + RMS_EPS) * g


def _gelu_exact(x):
    return 0.5 * x * (1.0 + lax.erf(x * (2.0 ** -0.5)))


def _ssm_prep_kernel(lam_re_ref, lam_im_ref, logdt_ref, bt_re_ref, bt_im_ref,
                     ct_re_ref, ct_im_ref,
                     abar_ref, wbu_ref, cmat_ref, apow_ref, inj_re_ref, inj_im_ref,
                     out_re_ref, out_imn_ref, kd_ref, wall_re_ref, wall_im_ref):
    lr = lam_re_ref[0]
    li = lam_im_ref[0]
    dt = jnp.exp(logdt_ref[0])
    mag = jnp.exp(lr * dt)
    ang = li * dt
    abar_r = mag * jnp.cos(ang)
    abar_i = mag * jnp.sin(ang)
    nr = abar_r - 1.0
    ni = abar_i
    den = lr * lr + li * li
    fr = (nr * lr + ni * li) / den
    fi = (ni * lr - nr * li) / den
    abar_ref[0, 0:1, :] = abar_r
    abar_ref[0, 1:2, :] = abar_i

    br = bt_re_ref[0]
    bi = bt_im_ref[0]
    bbar_r = fr * br - fi * bi
    bbar_i = fr * bi + fi * br
    ctr = ct_re_ref[0]
    cti = ct_im_ref[0]

    rows = lax.broadcasted_iota(jnp.int32, (MXU_DIM, CHUNK_STATE), 0)
    cols = lax.broadcasted_iota(jnp.int32, (MXU_DIM, CHUNK_STATE), 1)
    same_group = ((rows >> (SSM_GROUP.bit_length() - 1))
                  == (cols >> (SSM_STATE.bit_length() - 1)))

    def block_diag(src, k):
        piece = src[:, k * CHUNK_STATE:(k + 1) * CHUNK_STATE]
        tiled = jnp.concatenate([piece] * GROUPS_PER_CHUNK, axis=0)
        return jnp.where(same_group, tiled, 0.0)

    for k in range(N_CHUNKS):
        wbu_ref[0, k, :, 0:CHUNK_STATE] = block_diag(bbar_r, k).astype(_BF16)
        wbu_ref[0, k, :, CHUNK_STATE:2 * CHUNK_STATE] = block_diag(bbar_i, k).astype(_BF16)
        cmat_ref[0, k, 0:CHUNK_STATE, :] = block_diag(ctr, k).T.astype(_BF16)
        cmat_ref[0, k, CHUNK_STATE:2 * CHUNK_STATE, :] = block_diag(-cti, k).T.astype(_BF16)

    pr = jnp.ones_like(abar_r)
    pi = jnp.zeros_like(abar_r)
    for d in range(CHUNK_T + 1):
        wr = ctr * pr - cti * pi
        wi = ctr * pi + cti * pr
        if d < CHUNK_T:
            rs = slice(d * SSM_GROUP, (d + 1) * SSM_GROUP)
            wall_re_ref[rs, :] = wr
            wall_im_ref[rs, :] = wi
            s = CHUNK_T - 1 - d
            ss = slice(s * SSM_GROUP, (s + 1) * SSM_GROUP)
            inj_re_ref[0, ss, :] = (pr * bbar_r - pi * bbar_i).astype(_BF16)
            inj_im_ref[0, ss, :] = (pr * bbar_i + pi * bbar_r).astype(_BF16)
        if d >= 1:
            ts = slice((d - 1) * SSM_GROUP, d * SSM_GROUP)
            out_re_ref[0, ts, :] = wr.astype(_BF16)
            out_imn_ref[0, ts, :] = (-wi).astype(_BF16)
        if d == CHUNK_T:
            apow_ref[0, 0:1, :] = pr
            apow_ref[0, 1:2, :] = pi
            apow_ref[0, 2:3, :] = -pi
        pr, pi = pr * abar_r - pi * abar_i, pr * abar_i + pi * abar_r

    lane = lax.broadcasted_iota(jnp.int32, (SSM_GROUP, LANES), 1)
    groups_per_tile = LANES // SSM_STATE
    for k in range(GP // LANES):
        sl = slice(k * LANES, (k + 1) * LANES)
        wr_t = wall_re_ref[:, sl]
        wi_t = wall_im_ref[:, sl]
        for half in range(groups_per_tile):
            mine = (lane >> (SSM_STATE.bit_length() - 1)) == half
            kd = (_dot_nt(jnp.where(mine, bbar_r[:, sl], 0.0), wr_t)
                  - _dot_nt(jnp.where(mine, bbar_i[:, sl], 0.0), wi_t))
            g = k * groups_per_tile + half
            kd_ref[0, g * SSM_GROUP:(g + 1) * SSM_GROUP, :] = kd


def _ssm_prep(lam_re, lam_im, log_dt, b_re, b_im, c_re, c_im):
    lam_re = lam_re.reshape(DEPTH, 1, GP)
    lam_im = lam_im.reshape(DEPTH, 1, GP)
    logdt = jnp.repeat(log_dt, SSM_STATE, axis=-1).reshape(DEPTH, 1, GP)
    bt_re = b_re.transpose(0, 3, 1, 2).reshape(DEPTH, SSM_GROUP, GP)
    bt_im = b_im.transpose(0, 3, 1, 2).reshape(DEPTH, SSM_GROUP, GP)
    ct_re = c_re.transpose(0, 2, 1, 3).reshape(DEPTH, SSM_GROUP, GP)
    ct_im = c_im.transpose(0, 2, 1, 3).reshape(DEPTH, SSM_GROUP, GP)
    vec = pl.BlockSpec((1, 1, GP), lambda l: (l, 0, 0))
    mat = pl.BlockSpec((1, SSM_GROUP, GP), lambda l: (l, 0, 0))
    table = pl.BlockSpec((1, MXU_DIM, GP), lambda l: (l, 0, 0))
    abar, wbu, cmat, apow, inj_re, inj_im, out_re, out_imn, kd = pl.pallas_call(
        _ssm_prep_kernel,
        grid=(DEPTH,),
        in_specs=[vec, vec, vec, mat, mat, mat, mat],
        out_specs=[
            pl.BlockSpec((1, 2, GP), lambda l: (l, 0, 0)),
            pl.BlockSpec((1, N_CHUNKS, MXU_DIM, 2 * CHUNK_STATE), lambda l: (l, 0, 0, 0)),
            pl.BlockSpec((1, N_CHUNKS, 2 * CHUNK_STATE, MXU_DIM), lambda l: (l, 0, 0, 0)),
            pl.BlockSpec((1, 3, GP), lambda l: (l, 0, 0)),
            table, table, table, table,
            pl.BlockSpec((1, SSM_WIDTH, MXU_DIM), lambda l: (l, 0, 0)),
        ],
        out_shape=[
            jax.ShapeDtypeStruct((DEPTH, 2, GP), _F32),
            jax.ShapeDtypeStruct((DEPTH, N_CHUNKS, MXU_DIM, 2 * CHUNK_STATE), _BF16),
            jax.ShapeDtypeStruct((DEPTH, N_CHUNKS, 2 * CHUNK_STATE, MXU_DIM), _BF16),
            jax.ShapeDtypeStruct((DEPTH, 3, GP), _F32),
            jax.ShapeDtypeStruct((DEPTH, MXU_DIM, GP), _BF16),
            jax.ShapeDtypeStruct((DEPTH, MXU_DIM, GP), _BF16),
            jax.ShapeDtypeStruct((DEPTH, MXU_DIM, GP), _BF16),
            jax.ShapeDtypeStruct((DEPTH, MXU_DIM, GP), _BF16),
            jax.ShapeDtypeStruct((DEPTH, SSM_WIDTH, MXU_DIM), _F32),
        ],
        scratch_shapes=[pltpu.VMEM((MXU_DIM, GP), _F32), pltpu.VMEM((MXU_DIM, GP), _F32)],
        name="ssm_prep",
    )(lam_re, lam_im, logdt, bt_re, bt_im, ct_re, ct_im)

    def per_group(m):
        return m.reshape(DEPTH, MXU_DIM, SSM_GROUPS, SSM_STATE).transpose(0, 2, 1, 3)

    er, ei = per_group(inj_re), per_group(inj_im)
    s4 = jnp.concatenate([er, ei, ei, er], axis=-1)
    yin = jnp.concatenate([per_group(out_re).transpose(0, 1, 3, 2),
                           per_group(out_imn).transpose(0, 1, 3, 2)], axis=2)
    kd5 = kd.reshape(DEPTH, SSM_GROUPS, SSM_GROUP, CHUNK_T, SSM_GROUP)
    lag = jnp.arange(CHUNK_T)[None, :] - jnp.arange(CHUNK_T)[:, None]
    toep = jnp.take(kd5, jnp.maximum(lag, 0).reshape(-1), axis=3)
    toep = toep.reshape(DEPTH, SSM_GROUPS, SSM_GROUP, CHUNK_T, CHUNK_T, SSM_GROUP)
    toep = jnp.where((lag >= 0)[None, None, None, :, :, None], toep, 0.0)
    kin = toep.transpose(0, 1, 3, 2, 4, 5).reshape(DEPTH, SSM_GROUPS, MXU_DIM, MXU_DIM).astype(_BF16)
    ar, ai, ain = (apow[:, i].reshape(DEPTH, SSM_GROUPS, SSM_STATE) for i in range(3))
    apat = jnp.stack([jnp.concatenate([ar, ar], -1), jnp.concatenate([ain, ai], -1),
                      jnp.concatenate([ai, ain], -1)], axis=2)
    return abar, wbu, cmat, apat, s4, kin, yin


def _to_time_major(x_ref, slab_ref):
    batch, steps, width = x_ref.shape
    for n in range(batch):
        for j in range(width // LANES):
            slab_ref[j, pl.ds(n, steps, stride=batch), :] = x_ref[n, :, j * LANES:(j + 1) * LANES]
    return jnp.concatenate([slab_ref[j] for j in range(width // LANES)], axis=1)


def _from_time_major(x, slab_ref, o_ref):
    batch, steps, width = o_ref.shape
    for j in range(width // LANES):
        slab_ref[j] = x[:, j * LANES:(j + 1) * LANES]
    for n in range(batch):
        for j in range(width // LANES):
            o_ref[n, :, j * LANES:(j + 1) * LANES] = slab_ref[j, pl.ds(n, steps, stride=batch), :]


def _to_group_major(z_ref, u_ref):
    chunks = z_ref.shape[0]
    groups_per_tile = LANES // SSM_GROUP
    for q in range(SSM_WIDTH // LANES):
        for s in range(CHUNK_T):
            v = z_ref[:, s * SUBLANES:(s + 1) * SUBLANES, q * LANES:(q + 1) * LANES]
            v = v.reshape(chunks * SUBLANES, LANES)
            dst = (s % groups_per_tile) * SSM_GROUP
            for gl in range(groups_per_tile):
                g = q * groups_per_tile + gl
                r = pltpu.roll(v, (dst - gl * SSM_GROUP) % LANES, axis=1)
                c0 = g * MXU_DIM + (s // groups_per_tile) * LANES + dst
                u_ref[:, c0:c0 + SSM_GROUP] = r[:, dst:dst + SSM_GROUP]


def _from_group_major(y_ref, o_ref):
    chunks = o_ref.shape[0]
    groups_per_tile = LANES // SSM_GROUP
    for g in range(SSM_GROUPS):
        q, gl = divmod(g, groups_per_tile)
        dst = gl * SSM_GROUP
        for t in range(CHUNK_T):
            c0 = g * MXU_DIM + (t // groups_per_tile) * LANES
            v = y_ref[:, c0:c0 + LANES]
            r = pltpu.roll(v, (dst - (t % groups_per_tile) * SSM_GROUP) % LANES, axis=1)
            o_ref[:, t * SUBLANES:(t + 1) * SUBLANES, q * LANES + dst:q * LANES + dst + SSM_GROUP] = (
                r[:, dst:dst + SSM_GROUP].reshape(chunks, SUBLANES, SSM_GROUP))


def _cast_chunks(src_refs, dst_refs):
    for src, dst in zip(src_refs, dst_refs):
        dst[...] = src[...].astype(_BF16)


def _front_kernel(native_in, x_ref, g1_ref, win_ref, cw_ref, cb_ref,
                  ya_ref, zu_ref, u_ref, tail_ref, us_ref, *slab_ref):
    rows = ya_ref.shape[0]
    halo = (CONV_K - 1) * SUBLANES
    cw = CONV_WIDTH

    @pl.when(pl.program_id(0) == 0)
    def _():
        tail_ref[...] = jnp.zeros_like(tail_ref)

    x = _to_time_major(x_ref, slab_ref[0]) if native_in else x_ref[...]
    h = _rmsnorm(x, g1_ref[...]).astype(_BF16)
    zb = _dot(h, win_ref[:, 0:cw])
    zc = _dot(h, win_ref[:, cw:2 * cw])
    zv = _dot(h, win_ref[:, 2 * cw:3 * cw])
    cin = zc * zv
    pad = jnp.concatenate([tail_ref[...], cin], axis=0)
    conv = cb_ref[...]
    for k in range(CONV_K):
        conv = conv + cw_ref[k:k + 1, :] * pad[k * SUBLANES:k * SUBLANES + rows]
    tail_ref[...] = cin[rows - halo:rows]
    ya_ref[...] = (zb * conv).astype(_BF16)

    zu = _dot(h, win_ref[:, 3 * cw:3 * cw + SSM_WIDTH])
    zu_ref[...] = zu.reshape(zu_ref.shape)
    _to_group_major(zu_ref, us_ref)
    u_ref[...] = us_ref[...].astype(_BF16)


def _s5_chunk_kernel(u_ref, s4_ref, kin_ref, yin_ref, apat_ref, y_ref, st_ref, g4_ref, hin_ref):
    n_chunks = u_ref.shape[0] // SUBLANES
    for gi in range(S5_GROUPS_PER_STEP):
        cols = slice(gi * MXU_DIM, (gi + 1) * MXU_DIM)
        u = u_ref[:, cols]
        g4_ref[gi] = _dot(u, s4_ref[gi])
        a1 = jnp.broadcast_to(apat_ref[gi, 0:1, :], (SUBLANES, STATE2))
        a2 = jnp.broadcast_to(apat_ref[gi, 1:2, :], (SUBLANES, STATE2))
        a3 = jnp.broadcast_to(apat_ref[gi, 2:3, :], (SUBLANES, STATE2))
        p = jnp.zeros((SUBLANES, STATE2), _F32)
        q = jnp.zeros((SUBLANES, STATE2), _F32)
        for j in range(n_chunks):
            rs = pl.ds(j * SUBLANES, SUBLANES)
            hin_ref[gi, rs, :] = p
            p, q = (a1 * p + a2 * q + g4_ref[gi, rs, 0:STATE2],
                    a1 * q + a3 * p + g4_ref[gi, rs, STATE2:2 * STATE2])
        st_ref[gi] = p
        y_ref[:, cols] = (_dot(u, kin_ref[gi])
                          + _dot(hin_ref[gi].astype(_BF16), yin_ref[gi]))


N_BACK_WEIGHTS = 7


def _back_kernel(native_in, n_cast, x_ref, ya_ref, zu_ref, yg_ref, *refs):
    g1_ref, win_ref, d_ref, wglu_ref, bglu_ref, wbr_ref, wout_ref = refs[:N_BACK_WEIGHTS]
    refs = refs[N_BACK_WEIGHTS:]
    cast_src, xo_ref = refs[:n_cast], refs[n_cast]
    cast_dst, (ys_ref, *slab_ref) = refs[n_cast + 1:2 * n_cast + 1], refs[2 * n_cast + 1:]
    rows = xo_ref.shape[0]
    cw = CONV_WIDTH
    _cast_chunks(cast_src, cast_dst)

    x = _to_time_major(x_ref, slab_ref[0]) if native_in else x_ref[...]
    _from_group_major(yg_ref, ys_ref)
    zu = zu_ref[...].reshape(rows, SSM_WIDTH)
    y = ys_ref[...].reshape(rows, SSM_WIDTH) + d_ref[...] * zu
    gy = _gelu_exact(y)
    ys = gy * jax.nn.sigmoid(_dot(gy.astype(_BF16), wglu_ref[...]) + bglu_ref[...])

    oa = _dot(ya_ref[...], wbr_ref[0:cw, :])
    ob = _dot(ys.astype(_BF16), wbr_ref[cw:cw + SSM_WIDTH, :])
    h = _rmsnorm(x, g1_ref[...]).astype(_BF16)
    ga = _dot(h, win_ref[:, 0:D_MODEL])
    gs = _dot(h, win_ref[:, D_MODEL:2 * D_MODEL])
    m = jax.nn.sigmoid(ga) * oa + jax.nn.sigmoid(gs) * ob
    xo_ref[...] = x + _dot(m.astype(_BF16), wout_ref[...])


def _mixer_sample_kernel(x_ref, buf_ref, h0r_ref, h0i_ref, g1_ref, win_ref, cw_ref, cb_ref,
                         abar_ref, d_ref, wbu_ref, cmat_ref, wglu_ref, bglu_ref, wbr_ref, wout_ref,
                         xo_ref, cin_ref, str_ref, sti_ref):
    cw = CONV_WIDTH
    x = x_ref[...]
    h = _rmsnorm(x, g1_ref[...]).astype(_BF16)
    zb = _dot(h, win_ref[:, 0:cw])
    zc = _dot(h, win_ref[:, cw:2 * cw])
    zv = _dot(h, win_ref[:, 2 * cw:3 * cw])
    cin = zc * zv
    cin_ref[...] = cin
    conv = cb_ref[...]
    for k in range(CONV_K - 1):
        conv = conv + cw_ref[k:k + 1, :] * buf_ref[k]
    conv = conv + cw_ref[CONV_K - 1:CONV_K, :] * cin
    ya = (zb * conv).astype(_BF16)

    zu = _dot(h, win_ref[:, 3 * cw:3 * cw + SSM_WIDTH])
    zub = zu.astype(_BF16)
    ar = abar_ref[0:1, :]
    ai = abar_ref[1:2, :]
    h0r = h0r_ref[...]
    h0i = h0i_ref[...]
    ys = []
    for k in range(N_CHUNKS):
        sl = slice(k * CHUNK_STATE, (k + 1) * CHUNK_STATE)
        bu = _dot(zub[:, k * MXU_DIM:(k + 1) * MXU_DIM], wbu_ref[k])
        hr = ar[:, sl] * h0r[:, sl] - ai[:, sl] * h0i[:, sl] + bu[:, 0:CHUNK_STATE]
        hi = ar[:, sl] * h0i[:, sl] + ai[:, sl] * h0r[:, sl] + bu[:, CHUNK_STATE:2 * CHUNK_STATE]
        str_ref[:, sl] = hr
        sti_ref[:, sl] = hi
        ys.append(_dot(hr.astype(_BF16), cmat_ref[k, 0:CHUNK_STATE, :])
                  + _dot(hi.astype(_BF16), cmat_ref[k, CHUNK_STATE:2 * CHUNK_STATE, :]))
    y = jnp.concatenate(ys, axis=1) + d_ref[...] * zu
    gy = _gelu_exact(y)
    ys = gy * jax.nn.sigmoid(_dot(gy.astype(_BF16), wglu_ref[...]) + bglu_ref[...])

    oa = _dot(ya, wbr_ref[0:cw, :])
    ob = _dot(ys.astype(_BF16), wbr_ref[cw:cw + SSM_WIDTH, :])
    g0 = 3 * cw + SSM_WIDTH
    ga = _dot(h, win_ref[:, g0:g0 + D_MODEL])
    gs = _dot(h, win_ref[:, g0 + D_MODEL:g0 + 2 * D_MODEL])
    m = jax.nn.sigmoid(ga) * oa + jax.nn.sigmoid(gs) * ob
    xo_ref[...] = x + _dot(m.astype(_BF16), wout_ref[...])


def _resident(arr, layer):
    if arr.shape[0] == DEPTH and arr.ndim >= 3:
        shape = arr.shape[1:]
        return pl.BlockSpec((None,) + shape, lambda i: (layer,) + (0,) * len(shape),
                            pipeline_mode=pl.Buffered(1))
    return pl.BlockSpec(arr.shape, lambda i: (0,) * arr.ndim, pipeline_mode=pl.Buffered(1))


def _resident_cols(arr, half):
    rows, cols = arr.shape
    return pl.BlockSpec((rows, cols // 2), lambda i: (0, half), pipeline_mode=pl.Buffered(1))


def _cast_specs(arr, layer, n_steps):
    _, rows, cols = arr.shape
    hold = 1
    while (rows * hold) % (n_steps * BF16_SUBLANES):
        hold *= 2
    chunk = rows * hold // n_steps
    in_spec = pl.BlockSpec((None, chunk, cols), lambda i: (layer, i // hold, 0))
    out_spec = pl.BlockSpec((chunk, cols), lambda i: (i // hold, 0))
    return in_spec, out_spec, jax.ShapeDtypeStruct((rows, cols), _BF16)


def _slab_scratch(rows):
    return pltpu.VMEM((D_MODEL // LANES, rows, LANES), _F32)


def _prompt_x_spec(native_in):
    if native_in:
        return pl.BlockSpec((SUBLANES, ROW_BLOCK // SUBLANES, D_MODEL), lambda i: (0, i, 0))
    return pl.BlockSpec((ROW_BLOCK, D_MODEL), lambda i: (i, 0))


_ARBITRARY = pltpu.CompilerParams(dimension_semantics=("arbitrary",), vmem_limit_bytes=VMEM_LIMIT)


def _cast_kernel(n_cast, *refs):
    _cast_chunks(refs[:n_cast], refs[n_cast:])


def _cast_weights(arrs, layer, n_steps):
    specs = [_cast_specs(a, layer, n_steps) for a in arrs]
    return pl.pallas_call(
        functools.partial(_cast_kernel, len(arrs)),
        grid=(n_steps,),
        in_specs=[s[0] for s in specs],
        out_specs=[s[1] for s in specs],
        out_shape=[s[2] for s in specs],
        compiler_params=pltpu.CompilerParams(dimension_semantics=("arbitrary",)),
        name="cast_weights",
    )(*arrs)


def _mixer_front(layer, x, g1, w_in_b, conv_w, conv_b):
    native_in = x.ndim == 3
    rows = x.shape[0] * x.shape[1] if native_in else x.shape[0]
    chunks = rows // CHUNK_ROWS
    blk_chunks = ROW_BLOCK // CHUNK_ROWS
    halo = (CONV_K - 1) * SUBLANES
    return pl.pallas_call(
        functools.partial(_front_kernel, native_in),
        grid=(rows // ROW_BLOCK,),
        in_specs=[_prompt_x_spec(native_in), _resident(g1, layer), _resident_cols(w_in_b, 0),
                  _resident(conv_w, layer), _resident(conv_b, layer)],
        out_specs=[
            pl.BlockSpec((ROW_BLOCK, CONV_WIDTH), lambda i: (i, 0)),
            pl.BlockSpec((blk_chunks, CHUNK_ROWS, SSM_WIDTH), lambda i: (i, 0, 0)),
            pl.BlockSpec((blk_chunks * SUBLANES, GROUP_LANES), lambda i: (i, 0)),
            pl.BlockSpec((halo, CONV_WIDTH), lambda i: (0, 0)),
        ],
        out_shape=[
            jax.ShapeDtypeStruct((rows, CONV_WIDTH), _BF16),
            jax.ShapeDtypeStruct((chunks, CHUNK_ROWS, SSM_WIDTH), _F32),
            jax.ShapeDtypeStruct((chunks * SUBLANES, GROUP_LANES), _BF16),
            jax.ShapeDtypeStruct((halo, CONV_WIDTH), _F32),
        ],
        scratch_shapes=([pltpu.VMEM((blk_chunks * SUBLANES, GROUP_LANES), _F32)]
                        + ([_slab_scratch(ROW_BLOCK)] if native_in else [])),
        compiler_params=_ARBITRARY,
        name="mixer_front",
    )(x, g1, w_in_b, conv_w, conv_b)


def _s5_chunked(layer, u, s4, kin, yin, apat):
    rows = u.shape[0]
    gs = S5_GROUPS_PER_STEP
    wide = pl.BlockSpec((rows, gs * MXU_DIM), lambda i: (0, i))
    table = lambda r, c: pl.BlockSpec((None, gs, r, c), lambda i: (layer, i, 0, 0))
    return pl.pallas_call(
        _s5_chunk_kernel,
        grid=(SSM_GROUPS // gs,),
        in_specs=[wide, table(MXU_DIM, MXU_DIM), table(MXU_DIM, MXU_DIM), table(STATE2, MXU_DIM),
                  table(3, STATE2)],
        out_specs=[wide, pl.BlockSpec((gs, SUBLANES, STATE2), lambda i: (i, 0, 0))],
        out_shape=[jax.ShapeDtypeStruct((rows, GROUP_LANES), _F32),
                   jax.ShapeDtypeStruct((SSM_GROUPS, SUBLANES, STATE2), _F32)],
        scratch_shapes=[pltpu.VMEM((gs, rows, MXU_DIM), _F32), pltpu.VMEM((gs, rows, STATE2), _F32)],
        compiler_params=_ARBITRARY,
        name="s5_chunked",
    )(u, s4, kin, yin, apat)


def _mixer_back(layer, x, ya, zu, yg, weights, w_in_b, cast):
    native_in = x.ndim == 3
    rows = ya.shape[0]
    n_steps = rows // ROW_BLOCK
    blk_chunks = ROW_BLOCK // CHUNK_ROWS
    g1, d_skip, w_glu_b, bg, w_branch_b, w_out_b = weights
    row_spec = pl.BlockSpec((ROW_BLOCK, D_MODEL), lambda i: (i, 0))
    cast_specs = [_cast_specs(a, layer, n_steps) for a in cast]
    return pl.pallas_call(
        functools.partial(_back_kernel, native_in, len(cast)),
        grid=(n_steps,),
        in_specs=([_prompt_x_spec(native_in),
                   pl.BlockSpec((ROW_BLOCK, CONV_WIDTH), lambda i: (i, 0)),
                   pl.BlockSpec((blk_chunks, CHUNK_ROWS, SSM_WIDTH), lambda i: (i, 0, 0)),
                   pl.BlockSpec((blk_chunks * SUBLANES, GROUP_LANES), lambda i: (i, 0)),
                   _resident(g1, layer), _resident_cols(w_in_b, 1), _resident(d_skip, layer),
                   _resident(w_glu_b, layer), _resident(bg, layer), _resident(w_branch_b, layer),
                   _resident(w_out_b, layer)]
                  + [s[0] for s in cast_specs]),
        out_specs=[row_spec] + [s[1] for s in cast_specs],
        out_shape=[jax.ShapeDtypeStruct((rows, D_MODEL), _F32)] + [s[2] for s in cast_specs],
        scratch_shapes=([pltpu.VMEM((blk_chunks, CHUNK_ROWS, SSM_WIDTH), _F32)]
                        + ([_slab_scratch(ROW_BLOCK)] if native_in else [])),
        compiler_params=_ARBITRARY,
        name="mixer_back",
    )(x, ya, zu, yg, g1, w_in_b, d_skip, w_glu_b, bg, w_branch_b, w_out_b, *cast)


def _mixer_sample(layer, x, buf, h0r, h0i, weights):
    rows = x.shape[0]
    full = lambda shape: pl.BlockSpec(shape, lambda i: (0,) * len(shape))
    state = lambda shape: pl.BlockSpec((None,) + shape, lambda i: (layer,) + (0,) * len(shape))
    return pl.pallas_call(
        _mixer_sample_kernel,
        grid=(1,),
        in_specs=[full((rows, D_MODEL)), state((CONV_K - 1, rows, CONV_WIDTH)),
                  state((rows, GP)), state((rows, GP))] + [_resident(w, layer) for w in weights],
        out_specs=[full((rows, D_MODEL)), full((rows, CONV_WIDTH)),
                   full((rows, GP)), full((rows, GP))],
        out_shape=[
            jax.ShapeDtypeStruct((rows, D_MODEL), _F32),
            jax.ShapeDtypeStruct((rows, CONV_WIDTH), _F32),
            jax.ShapeDtypeStruct((rows, GP), _F32),
            jax.ShapeDtypeStruct((rows, GP), _F32),
        ],
        compiler_params=_ARBITRARY,
        name="mixer_sample",
    )(x, buf, h0r, h0i, *weights)


_FF_CHUNKS = ((0, 1024), (1024, 1024), (2048, 768))


def _ffn_kernel(final_norm, native_out, n_cast, x_ref, g2_ref, wgu_ref, wdown_ref, gf_ref, *refs):
    cast_src, xo_ref = refs[:n_cast], refs[n_cast]
    cast_dst, (act_ref, *slab_ref) = refs[n_cast + 1:2 * n_cast + 1], refs[2 * n_cast + 1:]
    _cast_chunks(cast_src, cast_dst)
    x = x_ref[...]
    h2 = _rmsnorm(x, g2_ref[...]).astype(_BF16)
    for c0, cn in _FF_CHUNKS:
        gate = _dot(h2, wgu_ref[:, c0:c0 + cn])
        up = _dot(h2, wgu_ref[:, D_FF + c0:D_FF + c0 + cn])
        act_ref[:, c0:c0 + cn] = (gate * jax.nn.sigmoid(gate) * up).astype(_BF16)
    xn = x + _dot(act_ref[...], wdown_ref[...])
    if final_norm:
        xn = _rmsnorm(xn, gf_ref[...])
    if native_out:
        _from_time_major(xn, slab_ref[0], xo_ref)
    else:
        xo_ref[...] = xn


def _ffn(layer, x, g2, wgu, wdown, gf, final_norm, native_out, cast=()):
    rows = x.shape[0]
    rb = min(FFN_ROW_BLOCK, rows)
    cast_specs = [_cast_specs(a, layer + 1, rows // rb) for a in cast]
    row_spec = pl.BlockSpec((rb, D_MODEL), lambda i: (i, 0))
    if native_out:
        out_spec = pl.BlockSpec((SUBLANES, rb // SUBLANES, D_MODEL), lambda i: (0, i, 0))
        out_shape = jax.ShapeDtypeStruct((SUBLANES, rows // SUBLANES, D_MODEL), _F32)
    else:
        out_spec = row_spec
        out_shape = jax.ShapeDtypeStruct((rows, D_MODEL), _F32)
    outs = pl.pallas_call(
        functools.partial(_ffn_kernel, final_norm, native_out, len(cast)),
        grid=(rows // rb,),
        in_specs=([row_spec, _resident(g2, layer), _resident(wgu, layer), _resident(wdown, layer),
                   _resident(gf, 0)] + [s[0] for s in cast_specs]),
        out_specs=[out_spec] + [s[1] for s in cast_specs],
        out_shape=[out_shape] + [s[2] for s in cast_specs],
        scratch_shapes=[pltpu.VMEM((rb, D_FF), _BF16)] + ([_slab_scratch(rb)] if native_out else []),
        compiler_params=_ARBITRARY,
        name="ffn",
    )(x, g2, wgu, wdown, gf, *cast)
    return outs if cast else outs[0]


def kernel(x_prompt, x_sample, state_conv, state_ssm_re, state_ssm_im, norm1_g, w_in, conv_w,
           conv_b, ssm_lam_re, ssm_lam_im, ssm_log_dt, ssm_b_re, ssm_b_im, ssm_c_re, ssm_c_im,
           ssm_d, w_glu, b_glu, w_branch, w_out, norm2_g, w_gate_up, w_down, final_g):
    batch, seq, _ = x_prompt.shape
    dec_batch = x_sample.shape[0]
    assert batch == SUBLANES and x_sample.shape[1] == 1

    abar, wbu, cmat, apat, s4, kin, yin = _ssm_prep(
        ssm_lam_re, ssm_lam_im, ssm_log_dt, ssm_b_re, ssm_b_im, ssm_c_re, ssm_c_im)
    g1 = norm1_g.reshape(DEPTH, 1, D_MODEL)
    g2 = norm2_g.reshape(DEPTH, 1, D_MODEL)
    gf = final_g.reshape(1, D_MODEL)
    cb = conv_b.reshape(DEPTH, 1, CONV_WIDTH)
    d_skip = ssm_d.reshape(DEPTH, 1, SSM_WIDTH)
    bg = b_glu.reshape(DEPTH, 1, SSM_WIDTH)
    mixer_f32 = (w_in, w_glu, w_branch, w_out)
    ffn_f32 = (w_gate_up, w_down)
    mixer_b = _cast_weights(mixer_f32, 0, CAST_STEPS)

    xp = x_prompt
    xs = x_sample.reshape(dec_batch, D_MODEL)
    bufs = state_conv.transpose(0, 2, 1, 3)
    h0r = state_ssm_re.reshape(DEPTH, dec_batch, GP)
    h0i = state_ssm_im.reshape(DEPTH, dec_batch, GP)

    p_conv, p_re, p_im, s_conv, s_re, s_im = [], [], [], [], [], []
    for l in range(DEPTH):
        last = l == DEPTH - 1
        w_in_b, w_glu_b, w_branch_b, w_out_b = mixer_b

        ya, zu, u, tail = _mixer_front(l, xp, g1, w_in_b, conv_w, cb)
        yg, st = _s5_chunked(l, u, s4, kin, yin, apat)
        xp, w_gate_up_b, w_down_b = _mixer_back(
            l, xp, ya, zu, yg, (g1, d_skip, w_glu_b, bg, w_branch_b, w_out_b), w_in_b, ffn_f32)
        if last:
            xp = _ffn(l, xp, g2, w_gate_up_b, w_down_b, gf, True, native_out=True)
        else:
            xp, *mixer_b = _ffn(l, xp, g2, w_gate_up_b, w_down_b, gf, False, native_out=False,
                                cast=mixer_f32)
        p_conv.append(tail.reshape(CONV_K - 1, batch, CONV_WIDTH).transpose(1, 0, 2))
        p_re.append(st[:, :, :SSM_STATE].transpose(1, 0, 2))
        p_im.append(st[:, :, SSM_STATE:].transpose(1, 0, 2))

        sample_weights = (g1, w_in_b, conv_w, cb, abar, d_skip, wbu, cmat, w_glu_b, bg,
                          w_branch_b, w_out_b)
        xs, cin, ssr, ssi = _mixer_sample(l, xs, bufs, h0r, h0i, sample_weights)
        xs = _ffn(l, xs, g2, w_gate_up_b, w_down_b, gf, last, native_out=False)
        s_conv.append(jnp.concatenate([state_conv[l][:, 1:], cin[:, None, :]], axis=1))
        s_re.append(ssr.reshape(dec_batch, SSM_GROUPS, SSM_STATE))
        s_im.append(ssi.reshape(dec_batch, SSM_GROUPS, SSM_STATE))

    y_prompt = xp
    y_sample = xs.reshape(dec_batch, 1, D_MODEL)
    return (y_prompt, y_sample, jnp.stack(p_conv), jnp.stack(p_re), jnp.stack(p_im),
            jnp.stack(s_conv), jnp.stack(s_re), jnp.stack(s_im))
```

```python
import functools

import jax
import jax.numpy as jnp
from jax import lax
from jax.experimental import pallas as pl
from jax.experimental.pallas import tpu as pltpu

D_MODEL = 1024
DEPTH = 4
CONV_WIDTH = 512
CONV_K = 3
SSM_WIDTH = 512
SSM_GROUP = 16
SSM_GROUPS = 32
SSM_STATE = 64
GP = SSM_GROUPS * SSM_STATE
D_FF = 2816
N_IN = 3 * CONV_WIDTH + SSM_WIDTH + 2 * D_MODEL
RMS_EPS = 1e-6

SUBLANES = 8
LANES = 128
BF16_SUBLANES = 16
CAST_STEPS = 8
MXU_DIM = 256
GROUPS_PER_CHUNK = MXU_DIM // SSM_GROUP
N_CHUNKS = SSM_GROUPS // GROUPS_PER_CHUNK
CHUNK_STATE = GROUPS_PER_CHUNK * SSM_STATE

CHUNK_T = MXU_DIM // SSM_GROUP
CHUNK_ROWS = CHUNK_T * SUBLANES
STATE2 = 2 * SSM_STATE
GROUP_LANES = SSM_GROUPS * MXU_DIM
S5_GROUPS_PER_STEP = 4

ROW_BLOCK = 512
FFN_ROW_BLOCK = 1024
VMEM_LIMIT = 56 * 1024 * 1024

_BF16 = jnp.bfloat16
_F32 = jnp.float32


def _dot(a, b):
    return jnp.dot(a, b, preferred_element_type=_F32)


def _dot_nt(a, b):
    return lax.dot_general(a, b, (((1,), (1,)), ((), ())), preferred_element_type=_F32)


def _rmsnorm(x, g):
    ms = jnp.mean(x * x, axis=-1, keepdims=True)
    return x * lax.rsqrt(ms + RMS_EPS) * g


def _gelu_exact(x):
    return 0.5 * x * (1.0 + lax.erf(x * (2.0 ** -0.5)))


def _ssm_prep_kernel(lam_re_ref, lam_im_ref, logdt_ref, bt_re_ref, bt_im_ref,
                     ct_re_ref, ct_im_ref,
                     abar_ref, wbu_ref, cmat_ref, apat_ref, s4_ref, kin_ref, yin_ref,
                     wall_re_ref, wall_im_ref, inj_re_ref, inj_im_ref, out_re_ref, out_imn_ref):
    lr = lam_re_ref[0]
    li = lam_im_ref[0]
    dt = jnp.exp(logdt_ref[0])
    mag = jnp.exp(lr * dt)
    ang = li * dt
    abar_r = mag * jnp.cos(ang)
    abar_i = mag * jnp.sin(ang)
    nr = abar_r - 1.0
    ni = abar_i
    den = lr * lr + li * li
    fr = (nr * lr + ni * li) / den
    fi = (ni * lr - nr * li) / den
    abar_ref[0, 0:1, :] = abar_r
    abar_ref[0, 1:2, :] = abar_i

    br = bt_re_ref[0]
    bi = bt_im_ref[0]
    bbar_r = fr * br - fi * bi
    bbar_i = fr * bi + fi * br
    ctr = ct_re_ref[0]
    cti = ct_im_ref[0]

    rows = lax.broadcasted_iota(jnp.int32, (MXU_DIM, CHUNK_STATE), 0)
    cols = lax.broadcasted_iota(jnp.int32, (MXU_DIM, CHUNK_STATE), 1)
    same_group = ((rows >> (SSM_GROUP.bit_length() - 1))
                  == (cols >> (SSM_STATE.bit_length() - 1)))

    def block_diag(src, k):
        piece = src[:, k * CHUNK_STATE:(k + 1) * CHUNK_STATE]
        tiled = jnp.concatenate([piece] * GROUPS_PER_CHUNK, axis=0)
        return jnp.where(same_group, tiled, 0.0)

    for k in range(N_CHUNKS):
        wbu_ref[0, k, :, 0:CHUNK_STATE] = block_diag(bbar_r, k).astype(_BF16)
        wbu_ref[0, k, :, CHUNK_STATE:2 * CHUNK_STATE] = block_diag(bbar_i, k).astype(_BF16)
        cmat_ref[0, k, 0:CHUNK_STATE, :] = block_diag(ctr, k).T.astype(_BF16)
        cmat_ref[0, k, CHUNK_STATE:2 * CHUNK_STATE, :] = block_diag(-cti, k).T.astype(_BF16)

    pr = jnp.ones_like(abar_r)
    pi = jnp.zeros_like(abar_r)
    for d in range(CHUNK_T + 1):
        wr = ctr * pr - cti * pi
        wi = ctr * pi + cti * pr
        if d < CHUNK_T:
            rs = slice(d * SSM_GROUP, (d + 1) * SSM_GROUP)
            wall_re_ref[rs, :] = wr
            wall_im_ref[rs, :] = wi
            s = CHUNK_T - 1 - d
            ss = slice(s * SSM_GROUP, (s + 1) * SSM_GROUP)
            inj_re_ref[ss, :] = pr * bbar_r - pi * bbar_i
            inj_im_ref[ss, :] = pr * bbar_i + pi * bbar_r
        if d >= 1:
            ts = slice((d - 1) * SSM_GROUP, d * SSM_GROUP)
            out_re_ref[ts, :] = wr
            out_imn_ref[ts, :] = -wi
        if d < CHUNK_T:
            pr, pi = pr * abar_r - pi * abar_i, pr * abar_i + pi * abar_r
    a_chunk_r, a_chunk_i = pr, pi

    assert LANES == 2 * SSM_STATE
    lane_g = lax.broadcasted_iota(jnp.int32, (SSM_GROUP, LANES), 1)
    lane_t = lax.broadcasted_iota(jnp.int32, (MXU_DIM, LANES), 1)
    lane_s = lax.broadcasted_iota(jnp.int32, (SUBLANES, LANES), 1)
    lane_k = lax.broadcasted_iota(jnp.int32, (SSM_GROUP, MXU_DIM), 1)
    swap = lambda v: pltpu.roll(v, SSM_STATE, axis=1)
    for k in range(GP // LANES):
        sl = slice(k * LANES, (k + 1) * LANES)
        wr_t = wall_re_ref[:, sl]
        wi_t = wall_im_ref[:, sl]
        er, ei = inj_re_ref[:, sl], inj_im_ref[:, sl]
        er_sw, ei_sw = swap(er), swap(ei)
        out_r_t = out_re_ref[:, sl].T
        out_i_t = out_imn_ref[:, sl].T
        ar8 = jnp.broadcast_to(a_chunk_r[:, sl], (SUBLANES, LANES))
        ai8 = jnp.broadcast_to(a_chunk_i[:, sl], (SUBLANES, LANES))
        for half in range(2):
            g = 2 * k + half
            lo_t = lane_t < SSM_STATE
            lo_s = lane_s < SSM_STATE
            if half == 0:
                first, second = jnp.where(lo_t, er, ei_sw), jnp.where(lo_t, ei, er_sw)
                ar_g, ai_g = jnp.where(lo_s, ar8, swap(ar8)), jnp.where(lo_s, ai8, swap(ai8))
            else:
                first, second = jnp.where(lo_t, er_sw, ei), jnp.where(lo_t, ei_sw, er)
                ar_g, ai_g = jnp.where(lo_s, swap(ar8), ar8), jnp.where(lo_s, swap(ai8), ai8)
            s4_ref[0, g, :, 0:LANES] = first.astype(_BF16)
            s4_ref[0, g, :, LANES:2 * LANES] = second.astype(_BF16)
            ps = slice(half * SSM_STATE, (half + 1) * SSM_STATE)
            yin_ref[0, g, 0:SSM_STATE, :] = out_r_t[ps].astype(_BF16)
            yin_ref[0, g, SSM_STATE:STATE2, :] = out_i_t[ps].astype(_BF16)
            apat_ref[0, g, 0] = ar_g
            apat_ref[0, g, 1] = jnp.where(lane_s < SSM_STATE, -ai_g, ai_g)
            apat_ref[0, g, 2] = jnp.where(lane_s < SSM_STATE, ai_g, -ai_g)
            mine = (lane_g < SSM_STATE) == (half == 0)
            kd = (_dot_nt(jnp.where(mine, bbar_r[:, sl], 0.0), wr_t)
                  - _dot_nt(jnp.where(mine, bbar_i[:, sl], 0.0), wi_t))
            for s in range(CHUNK_T):
                blk = kd if s == 0 else jnp.where(
                    lane_k >= s * SSM_GROUP, pltpu.roll(kd, s * SSM_GROUP, axis=1), 0.0)
                kin_ref[0, g, s * SSM_GROUP:(s + 1) * SSM_GROUP, :] = blk.astype(_BF16)


def _ssm_prep(lam_re, lam_im, log_dt, b_re, b_im, c_re, c_im):
    lam_re = lam_re.reshape(DEPTH, 1, GP)
    lam_im = lam_im.reshape(DEPTH, 1, GP)
    logdt = jnp.repeat(log_dt, SSM_STATE, axis=-1).reshape(DEPTH, 1, GP)
    bt_re = b_re.transpose(0, 3, 1, 2).reshape(DEPTH, SSM_GROUP, GP)
    bt_im = b_im.transpose(0, 3, 1, 2).reshape(DEPTH, SSM_GROUP, GP)
    ct_re = c_re.transpose(0, 2, 1, 3).reshape(DEPTH, SSM_GROUP, GP)
    ct_im = c_im.transpose(0, 2, 1, 3).reshape(DEPTH, SSM_GROUP, GP)
    vec = pl.BlockSpec((1, 1, GP), lambda l: (l, 0, 0))
    mat = pl.BlockSpec((1, SSM_GROUP, GP), lambda l: (l, 0, 0))
    per_group = lambda r, c: pl.BlockSpec((1, SSM_GROUPS, r, c), lambda l: (l, 0, 0, 0))
    return pl.pallas_call(
        _ssm_prep_kernel,
        grid=(DEPTH,),
        in_specs=[vec, vec, vec, mat, mat, mat, mat],
        out_specs=[
            pl.BlockSpec((1, 2, GP), lambda l: (l, 0, 0)),
            pl.BlockSpec((1, N_CHUNKS, MXU_DIM, 2 * CHUNK_STATE), lambda l: (l, 0, 0, 0)),
            pl.BlockSpec((1, N_CHUNKS, 2 * CHUNK_STATE, MXU_DIM), lambda l: (l, 0, 0, 0)),
            pl.BlockSpec((1, SSM_GROUPS, 3, SUBLANES, STATE2), lambda l: (l, 0, 0, 0, 0)),
            per_group(MXU_DIM, MXU_DIM), per_group(MXU_DIM, MXU_DIM), per_group(STATE2, MXU_DIM),
        ],
        out_shape=[
            jax.ShapeDtypeStruct((DEPTH, 2, GP), _F32),
            jax.ShapeDtypeStruct((DEPTH, N_CHUNKS, MXU_DIM, 2 * CHUNK_STATE), _BF16),
            jax.ShapeDtypeStruct((DEPTH, N_CHUNKS, 2 * CHUNK_STATE, MXU_DIM), _BF16),
            jax.ShapeDtypeStruct((DEPTH, SSM_GROUPS, 3, SUBLANES, STATE2), _F32),
            jax.ShapeDtypeStruct((DEPTH, SSM_GROUPS, MXU_DIM, MXU_DIM), _BF16),
            jax.ShapeDtypeStruct((DEPTH, SSM_GROUPS, MXU_DIM, MXU_DIM), _BF16),
            jax.ShapeDtypeStruct((DEPTH, SSM_GROUPS, STATE2, MXU_DIM), _BF16),
        ],
        scratch_shapes=[pltpu.VMEM((MXU_DIM, GP), _F32)] * 6,
        compiler_params=pltpu.CompilerParams(vmem_limit_bytes=VMEM_LIMIT),
        name="ssm_prep",
    )(lam_re, lam_im, logdt, bt_re, bt_im, ct_re, ct_im)


def _to_time_major(x_ref, slab_ref):
    batch, steps, width = x_ref.shape
    for n in range(batch):
        for j in range(width // LANES):
            slab_ref[j, pl.ds(n, steps, stride=batch), :] = x_ref[n, :, j * LANES:(j + 1) * LANES]
    return jnp.concatenate([slab_ref[j] for j in range(width // LANES)], axis=1)


def _from_time_major(x, slab_ref, o_ref):
    batch, steps, width = o_ref.shape
    for j in range(width // LANES):
        slab_ref[j] = x[:, j * LANES:(j + 1) * LANES]
    for n in range(batch):
        for j in range(width // LANES):
            o_ref[n, :, j * LANES:(j + 1) * LANES] = slab_ref[j, pl.ds(n, steps, stride=batch), :]


def _to_group_major(z_ref, u_ref):
    chunks = z_ref.shape[0]
    groups_per_tile = LANES // SSM_GROUP
    for q in range(SSM_WIDTH // LANES):
        for s in range(CHUNK_T):
            v = z_ref[:, s * SUBLANES:(s + 1) * SUBLANES, q * LANES:(q + 1) * LANES]
            v = v.reshape(chunks * SUBLANES, LANES)
            dst = (s % groups_per_tile) * SSM_GROUP
            for gl in range(groups_per_tile):
                g = q * groups_per_tile + gl
                r = pltpu.roll(v, (dst - gl * SSM_GROUP) % LANES, axis=1)
                c0 = g * MXU_DIM + (s // groups_per_tile) * LANES + dst
                u_ref[:, c0:c0 + SSM_GROUP] = r[:, dst:dst + SSM_GROUP]


def _from_group_major(y_ref, o_ref):
    chunks = o_ref.shape[0]
    groups_per_tile = LANES // SSM_GROUP
    for g in range(SSM_GROUPS):
        q, gl = divmod(g, groups_per_tile)
        dst = gl * SSM_GROUP
        for t in range(CHUNK_T):
            c0 = g * MXU_DIM + (t // groups_per_tile) * LANES
            v = y_ref[:, c0:c0 + LANES]
            r = pltpu.roll(v, (dst - (t % groups_per_tile) * SSM_GROUP) % LANES, axis=1)
            o_ref[:, t * SUBLANES:(t + 1) * SUBLANES, q * LANES + dst:q * LANES + dst + SSM_GROUP] = (
                r[:, dst:dst + SSM_GROUP].reshape(chunks, SUBLANES, SSM_GROUP))


def _cast_chunks(src_refs, dst_refs):
    for src, dst in zip(src_refs, dst_refs):
        dst[...] = src[...].astype(_BF16)


def _front_kernel(native_in, x_ref, g1_ref, win_ref, cw_ref, cb_ref,
                  ya_ref, zu_ref, u_ref, tail_ref, us_ref, *slab_ref):
    rows = ya_ref.shape[0]
    halo = (CONV_K - 1) * SUBLANES
    cw = CONV_WIDTH

    @pl.when(pl.program_id(0) == 0)
    def _():
        tail_ref[...] = jnp.zeros_like(tail_ref)

    x = _to_time_major(x_ref, slab_ref[0]) if native_in else x_ref[...]
    h = _rmsnorm(x, g1_ref[...]).astype(_BF16)
    zu = _dot(h, win_ref[:, 3 * cw:3 * cw + SSM_WIDTH])
    zu_ref[...] = zu.reshape(zu_ref.shape)
    _to_group_major(zu_ref, us_ref)
    u_ref[...] = us_ref[...].astype(_BF16)

    zb = _dot(h, win_ref[:, 0:cw])
    zc = _dot(h, win_ref[:, cw:2 * cw])
    zv = _dot(h, win_ref[:, 2 * cw:3 * cw])
    cin = zc * zv
    pad = jnp.concatenate([tail_ref[...], cin], axis=0)
    conv = cb_ref[...]
    for k in range(CONV_K):
        conv = conv + cw_ref[k:k + 1, :] * pad[k * SUBLANES:k * SUBLANES + rows]
    tail_ref[...] = cin[rows - halo:rows]
    ya_ref[...] = (zb * conv).astype(_BF16)


def _s5_chunk_kernel(u_ref, s4_ref, kin_ref, yin_ref, apat_ref, y_ref, st_ref, g4_ref, hin_ref):
    n_chunks = u_ref.shape[0] // SUBLANES
    for gi in range(S5_GROUPS_PER_STEP):
        cols = slice(gi * MXU_DIM, (gi + 1) * MXU_DIM)
        u = u_ref[:, cols]
        g4_ref[gi] = _dot(u, s4_ref[gi])
        a1, a2, a3 = apat_ref[gi, 0], apat_ref[gi, 1], apat_ref[gi, 2]
        p = jnp.zeros((SUBLANES, STATE2), _F32)
        q = jnp.zeros((SUBLANES, STATE2), _F32)
        for j in range(n_chunks):
            rs = pl.ds(j * SUBLANES, SUBLANES)
            hin_ref[gi, rs, :] = p
            p, q = (a1 * p + a2 * q + g4_ref[gi, rs, 0:STATE2],
                    a1 * q + a3 * p + g4_ref[gi, rs, STATE2:2 * STATE2])
        st_ref[gi] = p
        y_ref[:, cols] = (_dot(u, kin_ref[gi])
                          + _dot(hin_ref[gi].astype(_BF16), yin_ref[gi]))


N_BACK_WEIGHTS = 7


def _back_kernel(native_in, n_cast, x_ref, ya_ref, zu_ref, yg_ref, *refs):
    g1_ref, win_ref, d_ref, wglu_ref, bglu_ref, wbr_ref, wout_ref = refs[:N_BACK_WEIGHTS]
    refs = refs[N_BACK_WEIGHTS:]
    cast_src, xo_ref = refs[:n_cast], refs[n_cast]
    cast_dst, (ys_ref, *slab_ref) = refs[n_cast + 1:2 * n_cast + 1], refs[2 * n_cast + 1:]
    rows = xo_ref.shape[0]
    cw = CONV_WIDTH
    _cast_chunks(cast_src, cast_dst)

    x = _to_time_major(x_ref, slab_ref[0]) if native_in else x_ref[...]
    h = _rmsnorm(x, g1_ref[...]).astype(_BF16)
    ga = _dot(h, win_ref[:, 0:D_MODEL])
    gs = _dot(h, win_ref[:, D_MODEL:2 * D_MODEL])
    oa = _dot(ya_ref[...], wbr_ref[0:cw, :])

    _from_group_major(yg_ref, ys_ref)
    zu = zu_ref[...].reshape(rows, SSM_WIDTH)
    y = ys_ref[...].reshape(rows, SSM_WIDTH) + d_ref[...] * zu
    gy = _gelu_exact(y)
    ys = gy * jax.nn.sigmoid(_dot(gy.astype(_BF16), wglu_ref[...]) + bglu_ref[...])
    ob = _dot(ys.astype(_BF16), wbr_ref[cw:cw + SSM_WIDTH, :])
    m = jax.nn.sigmoid(ga) * oa + jax.nn.sigmoid(gs) * ob
    xo_ref[...] = x + _dot(m.astype(_BF16), wout_ref[...])


def _mixer_sample_kernel(x_ref, buf_ref, h0r_ref, h0i_ref, g1_ref, win_ref, cw_ref, cb_ref,
                         abar_ref, d_ref, wbu_ref, cmat_ref, wglu_ref, bglu_ref, wbr_ref, wout_ref,
                         xo_ref, cin_ref, str_ref, sti_ref):
    cw = CONV_WIDTH
    x = x_ref[...]
    h = _rmsnorm(x, g1_ref[...]).astype(_BF16)
    zb = _dot(h, win_ref[:, 0:cw])
    zc = _dot(h, win_ref[:, cw:2 * cw])
    zv = _dot(h, win_ref[:, 2 * cw:3 * cw])
    cin = zc * zv
    cin_ref[...] = cin
    conv = cb_ref[...]
    for k in range(CONV_K - 1):
        conv = conv + cw_ref[k:k + 1, :] * buf_ref[k]
    conv = conv + cw_ref[CONV_K - 1:CONV_K, :] * cin
    ya = (zb * conv).astype(_BF16)

    zu = _dot(h, win_ref[:, 3 * cw:3 * cw + SSM_WIDTH])
    zub = zu.astype(_BF16)
    ar = abar_ref[0:1, :]
    ai = abar_ref[1:2, :]
    h0r = h0r_ref[...]
    h0i = h0i_ref[...]
    ys = []
    for k in range(N_CHUNKS):
        sl = slice(k * CHUNK_STATE, (k + 1) * CHUNK_STATE)
        bu = _dot(zub[:, k * MXU_DIM:(k + 1) * MXU_DIM], wbu_ref[k])
        hr = ar[:, sl] * h0r[:, sl] - ai[:, sl] * h0i[:, sl] + bu[:, 0:CHUNK_STATE]
        hi = ar[:, sl] * h0i[:, sl] + ai[:, sl] * h0r[:, sl] + bu[:, CHUNK_STATE:2 * CHUNK_STATE]
        str_ref[:, sl] = hr
        sti_ref[:, sl] = hi
        ys.append(_dot(hr.astype(_BF16), cmat_ref[k, 0:CHUNK_STATE, :])
                  + _dot(hi.astype(_BF16), cmat_ref[k, CHUNK_STATE:2 * CHUNK_STATE, :]))
    y = jnp.concatenate(ys, axis=1) + d_ref[...] * zu
    gy = _gelu_exact(y)
    ys = gy * jax.nn.sigmoid(_dot(gy.astype(_BF16), wglu_ref[...]) + bglu_ref[...])

    oa = _dot(ya, wbr_ref[0:cw, :])
    ob = _dot(ys.astype(_BF16), wbr_ref[cw:cw + SSM_WIDTH, :])
    g0 = 3 * cw + SSM_WIDTH
    ga = _dot(h, win_ref[:, g0:g0 + D_MODEL])
    gs = _dot(h, win_ref[:, g0 + D_MODEL:g0 + 2 * D_MODEL])
    m = jax.nn.sigmoid(ga) * oa + jax.nn.sigmoid(gs) * ob
    xo_ref[...] = x + _dot(m.astype(_BF16), wout_ref[...])


def _resident(arr, layer):
    if arr.shape[0] == DEPTH and arr.ndim >= 3:
        shape = arr.shape[1:]
        return pl.BlockSpec((None,) + shape, lambda i: (layer,) + (0,) * len(shape),
                            pipeline_mode=pl.Buffered(1))
    return pl.BlockSpec(arr.shape, lambda i: (0,) * arr.ndim, pipeline_mode=pl.Buffered(1))


def _resident_cols(arr, half):
    rows, cols = arr.shape
    return pl.BlockSpec((rows, cols // 2), lambda i: (0, half), pipeline_mode=pl.Buffered(1))


def _cast_specs(arr, layer, n_steps):
    _, rows, cols = arr.shape
    hold = 1
    while (rows * hold) % (n_steps * BF16_SUBLANES):
        hold *= 2
    chunk = rows * hold // n_steps
    in_spec = pl.BlockSpec((None, chunk, cols), lambda i: (layer, i // hold, 0))
    out_spec = pl.BlockSpec((chunk, cols), lambda i: (i // hold, 0))
    return in_spec, out_spec, jax.ShapeDtypeStruct((rows, cols), _BF16)


def _slab_scratch(rows):
    return pltpu.VMEM((D_MODEL // LANES, rows, LANES), _F32)


def _prompt_x_spec(native_in):
    if native_in:
        return pl.BlockSpec((SUBLANES, ROW_BLOCK // SUBLANES, D_MODEL), lambda i: (0, i, 0))
    return pl.BlockSpec((ROW_BLOCK, D_MODEL), lambda i: (i, 0))


_ARBITRARY = pltpu.CompilerParams(dimension_semantics=("arbitrary",), vmem_limit_bytes=VMEM_LIMIT)


def _cast_kernel(n_cast, *refs):
    _cast_chunks(refs[:n_cast], refs[n_cast:])


def _cast_weights(arrs, layer, n_steps):
    specs = [_cast_specs(a, layer, n_steps) for a in arrs]
    return pl.pallas_call(
        functools.partial(_cast_kernel, len(arrs)),
        grid=(n_steps,),
        in_specs=[s[0] for s in specs],
        out_specs=[s[1] for s in specs],
        out_shape=[s[2] for s in specs],
        compiler_params=pltpu.CompilerParams(dimension_semantics=("arbitrary",)),
        name="cast_weights",
    )(*arrs)


def _mixer_front(layer, x, g1, w_in_b, conv_w, conv_b):
    native_in = x.ndim == 3
    rows = x.shape[0] * x.shape[1] if native_in else x.shape[0]
    chunks = rows // CHUNK_ROWS
    blk_chunks = ROW_BLOCK // CHUNK_ROWS
    halo = (CONV_K - 1) * SUBLANES
    return pl.pallas_call(
        functools.partial(_front_kernel, native_in),
        grid=(rows // ROW_BLOCK,),
        in_specs=[_prompt_x_spec(native_in), _resident(g1, layer), _resident_cols(w_in_b, 0),
                  _resident(conv_w, layer), _resident(conv_b, layer)],
        out_specs=[
            pl.BlockSpec((ROW_BLOCK, CONV_WIDTH), lambda i: (i, 0)),
            pl.BlockSpec((blk_chunks, CHUNK_ROWS, SSM_WIDTH), lambda i: (i, 0, 0)),
            pl.BlockSpec((blk_chunks * SUBLANES, GROUP_LANES), lambda i: (i, 0)),
            pl.BlockSpec((halo, CONV_WIDTH), lambda i: (0, 0)),
        ],
        out_shape=[
            jax.ShapeDtypeStruct((rows, CONV_WIDTH), _BF16),
            jax.ShapeDtypeStruct((chunks, CHUNK_ROWS, SSM_WIDTH), _F32),
            jax.ShapeDtypeStruct((chunks * SUBLANES, GROUP_LANES), _BF16),
            jax.ShapeDtypeStruct((halo, CONV_WIDTH), _F32),
        ],
        scratch_shapes=([pltpu.VMEM((blk_chunks * SUBLANES, GROUP_LANES), _F32)]
                        + ([_slab_scratch(ROW_BLOCK)] if native_in else [])),
        compiler_params=_ARBITRARY,
        name="mixer_front",
    )(x, g1, w_in_b, conv_w, conv_b)


def _s5_chunked(layer, u, s4, kin, yin, apat):
    rows = u.shape[0]
    gs = S5_GROUPS_PER_STEP
    wide = pl.BlockSpec((rows, gs * MXU_DIM), lambda i: (0, i))
    table = lambda r, c: pl.BlockSpec((None, gs, r, c), lambda i: (layer, i, 0, 0))
    return pl.pallas_call(
        _s5_chunk_kernel,
        grid=(SSM_GROUPS // gs,),
        in_specs=[wide, table(MXU_DIM, MXU_DIM), table(MXU_DIM, MXU_DIM), table(STATE2, MXU_DIM),
                  pl.BlockSpec((None, gs, 3, SUBLANES, STATE2), lambda i: (layer, i, 0, 0, 0))],
        out_specs=[wide, pl.BlockSpec((gs, SUBLANES, STATE2), lambda i: (i, 0, 0))],
        out_shape=[jax.ShapeDtypeStruct((rows, GROUP_LANES), _F32),
                   jax.ShapeDtypeStruct((SSM_GROUPS, SUBLANES, STATE2), _F32)],
        scratch_shapes=[pltpu.VMEM((gs, rows, MXU_DIM), _F32), pltpu.VMEM((gs, rows, STATE2), _F32)],
        compiler_params=_ARBITRARY,
        name="s5_chunked",
    )(u, s4, kin, yin, apat)


def _mixer_back(layer, x, ya, zu, yg, weights, w_in_b, cast):
    native_in = x.ndim == 3
    rows = ya.shape[0]
    n_steps = rows // ROW_BLOCK
    blk_chunks = ROW_BLOCK // CHUNK_ROWS
    g1, d_skip, w_glu_b, bg, w_branch_b, w_out_b = weights
    row_spec = pl.BlockSpec((ROW_BLOCK, D_MODEL), lambda i: (i, 0))
    cast_specs = [_cast_specs(a, layer, n_steps) for a in cast]
    return pl.pallas_call(
        functools.partial(_back_kernel, native_in, len(cast)),
        grid=(n_steps,),
        in_specs=([_prompt_x_spec(native_in),
                   pl.BlockSpec((ROW_BLOCK, CONV_WIDTH), lambda i: (i, 0)),
                   pl.BlockSpec((blk_chunks, CHUNK_ROWS, SSM_WIDTH), lambda i: (i, 0, 0)),
                   pl.BlockSpec((blk_chunks * SUBLANES, GROUP_LANES), lambda i: (i, 0)),
                   _resident(g1, layer), _resident_cols(w_in_b, 1), _resident(d_skip, layer),
                   _resident(w_glu_b, layer), _resident(bg, layer), _resident(w_branch_b, layer),
                   _resident(w_out_b, layer)]
                  + [s[0] for s in cast_specs]),
        out_specs=[row_spec] + [s[1] for s in cast_specs],
        out_shape=[jax.ShapeDtypeStruct((rows, D_MODEL), _F32)] + [s[2] for s in cast_specs],
        scratch_shapes=([pltpu.VMEM((blk_chunks, CHUNK_ROWS, SSM_WIDTH), _F32)]
                        + ([_slab_scratch(ROW_BLOCK)] if native_in else [])),
        compiler_params=_ARBITRARY,
        name="mixer_back",
    )(x, ya, zu, yg, g1, w_in_b, d_skip, w_glu_b, bg, w_branch_b, w_out_b, *cast)


def _mixer_sample(layer, x, buf, h0r, h0i, weights):
    rows = x.shape[0]
    full = lambda shape: pl.BlockSpec(shape, lambda i: (0,) * len(shape))
    state = lambda shape: pl.BlockSpec((None,) + shape, lambda i: (layer,) + (0,) * len(shape))
    return pl.pallas_call(
        _mixer_sample_kernel,
        grid=(1,),
        in_specs=[full((rows, D_MODEL)), state((CONV_K - 1, rows, CONV_WIDTH)),
                  state((rows, GP)), state((rows, GP))] + [_resident(w, layer) for w in weights],
        out_specs=[full((rows, D_MODEL)), full((rows, CONV_WIDTH)),
                   full((rows, GP)), full((rows, GP))],
        out_shape=[
            jax.ShapeDtypeStruct((rows, D_MODEL), _F32),
            jax.ShapeDtypeStruct((rows, CONV_WIDTH), _F32),
            jax.ShapeDtypeStruct((rows, GP), _F32),
            jax.ShapeDtypeStruct((rows, GP), _F32),
        ],
        compiler_params=_ARBITRARY,
        name="mixer_sample",
    )(x, buf, h0r, h0i, *weights)


_FF_CHUNKS = ((0, 1024), (1024, 1024), (2048, 768))


def _ffn_kernel(final_norm, native_out, n_cast, x_ref, g2_ref, wgu_ref, wdown_ref, gf_ref, *refs):
    cast_src, xo_ref = refs[:n_cast], refs[n_cast]
    cast_dst, (act_ref, *slab_ref) = refs[n_cast + 1:2 * n_cast + 1], refs[2 * n_cast + 1:]
    _cast_chunks(cast_src, cast_dst)
    x = x_ref[...]
    h2 = _rmsnorm(x, g2_ref[...]).astype(_BF16)
    for c0, cn in _FF_CHUNKS:
        gate = _dot(h2, wgu_ref[:, c0:c0 + cn])
        up = _dot(h2, wgu_ref[:, D_FF + c0:D_FF + c0 + cn])
        act_ref[:, c0:c0 + cn] = (gate * jax.nn.sigmoid(gate) * up).astype(_BF16)
    xn = x + _dot(act_ref[...], wdown_ref[...])
    if final_norm:
        xn = _rmsnorm(xn, gf_ref[...])
    if native_out:
        _from_time_major(xn, slab_ref[0], xo_ref)
    else:
        xo_ref[...] = xn


def _ffn(layer, x, g2, wgu, wdown, gf, final_norm, native_out, cast=()):
    rows = x.shape[0]
    rb = min(FFN_ROW_BLOCK, rows)
    cast_specs = [_cast_specs(a, layer + 1, rows // rb) for a in cast]
    row_spec = pl.BlockSpec((rb, D_MODEL), lambda i: (i, 0))
    if native_out:
        out_spec = pl.BlockSpec((SUBLANES, rb // SUBLANES, D_MODEL), lambda i: (0, i, 0))
        out_shape = jax.ShapeDtypeStruct((SUBLANES, rows // SUBLANES, D_MODEL), _F32)
    else:
        out_spec = row_spec
        out_shape = jax.ShapeDtypeStruct((rows, D_MODEL), _F32)
    outs = pl.pallas_call(
        functools.partial(_ffn_kernel, final_norm, native_out, len(cast)),
        grid=(rows // rb,),
        in_specs=([row_spec, _resident(g2, layer), _resident(wgu, layer), _resident(wdown, layer),
                   _resident(gf, 0)] + [s[0] for s in cast_specs]),
        out_specs=[out_spec] + [s[1] for s in cast_specs],
        out_shape=[out_shape] + [s[2] for s in cast_specs],
        scratch_shapes=[pltpu.VMEM((rb, D_FF), _BF16)] + ([_slab_scratch(rb)] if native_out else []),
        compiler_params=_ARBITRARY,
        name="ffn",
    )(x, g2, wgu, wdown, gf, *cast)
    return outs if cast else outs[0]


def kernel(x_prompt, x_sample, state_conv, state_ssm_re, state_ssm_im, norm1_g, w_in, conv_w,
           conv_b, ssm_lam_re, ssm_lam_im, ssm_log_dt, ssm_b_re, ssm_b_im, ssm_c_re, ssm_c_im,
           ssm_d, w_glu, b_glu, w_branch, w_out, norm2_g, w_gate_up, w_down, final_g):
    batch, seq, _ = x_prompt.shape
    dec_batch = x_sample.shape[0]
    assert batch == SUBLANES and x_sample.shape[1] == 1

    abar, wbu, cmat, apat, s4, kin, yin = _ssm_prep(
        ssm_lam_re, ssm_lam_im, ssm_log_dt, ssm_b_re, ssm_b_im, ssm_c_re, ssm_c_im)
    g1 = norm1_g.reshape(DEPTH, 1, D_MODEL)
    g2 = norm2_g.reshape(DEPTH, 1, D_MODEL)
    gf = final_g.reshape(1, D_MODEL)
    cb = conv_b.reshape(DEPTH, 1, CONV_WIDTH)
    d_skip = ssm_d.reshape(DEPTH, 1, SSM_WIDTH)
    bg = b_glu.reshape(DEPTH, 1, SSM_WIDTH)
    mixer_f32 = (w_in, w_glu, w_branch, w_out)
    ffn_f32 = (w_gate_up, w_down)
    mixer_b = _cast_weights(mixer_f32, 0, CAST_STEPS)

    xp = x_prompt
    xs = x_sample.reshape(dec_batch, D_MODEL)
    bufs = state_conv.transpose(0, 2, 1, 3)
    h0r = state_ssm_re.reshape(DEPTH, dec_batch, GP)
    h0i = state_ssm_im.reshape(DEPTH, dec_batch, GP)

    p_conv, p_re, p_im, s_conv, s_re, s_im = [], [], [], [], [], []
    for l in range(DEPTH):
        last = l == DEPTH - 1
        w_in_b, w_glu_b, w_branch_b, w_out_b = mixer_b

        ya, zu, u, tail = _mixer_front(l, xp, g1, w_in_b, conv_w, cb)
        yg, st = _s5_chunked(l, u, s4, kin, yin, apat)
        xp, w_gate_up_b, w_down_b = _mixer_back(
            l, xp, ya, zu, yg, (g1, d_skip, w_glu_b, bg, w_branch_b, w_out_b), w_in_b, ffn_f32)
        if last:
            xp = _ffn(l, xp, g2, w_gate_up_b, w_down_b, gf, True, native_out=True)
        else:
            xp, *mixer_b = _ffn(l, xp, g2, w_gate_up_b, w_down_b, gf, False, native_out=False,
                                cast=mixer_f32)
        p_conv.append(tail.reshape(CONV_K - 1, batch, CONV_WIDTH).transpose(1, 0, 2))
        p_re.append(st[:, :, :SSM_STATE].transpose(1, 0, 2))
        p_im.append(st[:, :, SSM_STATE:].transpose(1, 0, 2))

        sample_weights = (g1, w_in_b, conv_w, cb, abar, d_skip, wbu, cmat, w_glu_b, bg,
                          w_branch_b, w_out_b)
        xs, cin, ssr, ssi = _mixer_sample(l, xs, bufs, h0r, h0i, sample_weights)
        xs = _ffn(l, xs, g2, w_gate_up_b, w_down_b, gf, last, native_out=False)
        s_conv.append(jnp.concatenate([state_conv[l][:, 1:], cin[:, None, :]], axis=1))
        s_re.append(ssr.reshape(dec_batch, SSM_GROUPS, SSM_STATE))
        s_im.append(ssi.reshape(dec_batch, SSM_GROUPS, SSM_STATE))

    y_prompt = xp
    y_sample = xs.reshape(dec_batch, 1, D_MODEL)
    return (y_prompt, y_sample, jnp.stack(p_conv), jnp.stack(p_re), jnp.stack(p_im),
            jnp.stack(s_conv), jnp.stack(s_re), jnp.stack(s_im))
```

```python
import functools

import jax
import jax.numpy as jnp
from jax import lax
from jax.experimental import pallas as pl
from jax.experimental.pallas import tpu as pltpu

D_MODEL = 1024
DEPTH = 4
CONV_WIDTH = 512
CONV_K = 3
SSM_WIDTH = 512
SSM_GROUP = 16
SSM_GROUPS = 32
SSM_STATE = 64
GP = SSM_GROUPS * SSM_STATE
D_FF = 2816
N_IN = 3 * CONV_WIDTH + SSM_WIDTH + 2 * D_MODEL
RMS_EPS = 1e-6

SUBLANES = 8
LANES = 128
BF16_SUBLANES = 16
CAST_STEPS = 8
MXU_DIM = 256
GROUPS_PER_CHUNK = MXU_DIM // SSM_GROUP
N_CHUNKS = SSM_GROUPS // GROUPS_PER_CHUNK
CHUNK_STATE = GROUPS_PER_CHUNK * SSM_STATE

CHUNK_T = MXU_DIM // SSM_GROUP
CHUNK_ROWS = CHUNK_T * SUBLANES
STATE2 = 2 * SSM_STATE
GROUP_LANES = SSM_GROUPS * MXU_DIM
S5_GROUPS_PER_STEP = 4

ROW_BLOCK = 1024
FFN_ROW_BLOCK = 1024
VMEM_LIMIT = 56 * 1024 * 1024

_BF16 = jnp.bfloat16
_F32 = jnp.float32


def _dot(a, b):
    return jnp.dot(a, b, preferred_element_type=_F32)


def _dot_nt(a, b):
    return lax.dot_general(a, b, (((1,), (1,)), ((), ())), preferred_element_type=_F32)


def _rmsnorm(x, g):
    ms = jnp.mean(x * x, axis=-1, keepdims=True)
    return x * lax.rsqrt(ms + RMS_EPS) * g


def _gelu_exact(x):
    return 0.5 * x * (1.0 + lax.erf(x * (2.0 ** -0.5)))


def _ssm_prep_kernel(lam_re_ref, lam_im_ref, logdt_ref, bt_re_ref, bt_im_ref,
                     ct_re_ref, ct_im_ref,
                     abar_ref, wbu_ref, cmat_ref, apat_ref, s4_ref, kin_ref, yin_ref,
                     wall_re_ref, wall_im_ref, inj_re_ref, inj_im_ref, out_re_ref, out_imn_ref):
    lr = lam_re_ref[0]
    li = lam_im_ref[0]
    dt = jnp.exp(logdt_ref[0])
    mag = jnp.exp(lr * dt)
    ang = li * dt
    abar_r = mag * jnp.cos(ang)
    abar_i = mag * jnp.sin(ang)
    nr = abar_r - 1.0
    ni = abar_i
    den = lr * lr + li * li
    fr = (nr * lr + ni * li) / den
    fi = (ni * lr - nr * li) / den
    abar_ref[0, 0:1, :] = abar_r
    abar_ref[0, 1:2, :] = abar_i

    br = bt_re_ref[0]
    bi = bt_im_ref[0]
    bbar_r = fr * br - fi * bi
    bbar_i = fr * bi + fi * br
    ctr = ct_re_ref[0]
    cti = ct_im_ref[0]

    rows = lax.broadcasted_iota(jnp.int32, (MXU_DIM, CHUNK_STATE), 0)
    cols = lax.broadcasted_iota(jnp.int32, (MXU_DIM, CHUNK_STATE), 1)
    same_group = ((rows >> (SSM_GROUP.bit_length() - 1))
                  == (cols >> (SSM_STATE.bit_length() - 1)))

    def block_diag(src, k):
        piece = src[:, k * CHUNK_STATE:(k + 1) * CHUNK_STATE]
        tiled = jnp.concatenate([piece] * GROUPS_PER_CHUNK, axis=0)
        return jnp.where(same_group, tiled, 0.0)

    for k in range(N_CHUNKS):
        wbu_ref[0, k, :, 0:CHUNK_STATE] = block_diag(bbar_r, k).astype(_BF16)
        wbu_ref[0, k, :, CHUNK_STATE:2 * CHUNK_STATE] = block_diag(bbar_i, k).astype(_BF16)
        cmat_ref[0, k, 0:CHUNK_STATE, :] = block_diag(ctr, k).T.astype(_BF16)
        cmat_ref[0, k, CHUNK_STATE:2 * CHUNK_STATE, :] = block_diag(-cti, k).T.astype(_BF16)

    pr = jnp.ones_like(abar_r)
    pi = jnp.zeros_like(abar_r)
    for d in range(CHUNK_T + 1):
        wr = ctr * pr - cti * pi
        wi = ctr * pi + cti * pr
        if d < CHUNK_T:
            rs = slice(d * SSM_GROUP, (d + 1) * SSM_GROUP)
            wall_re_ref[rs, :] = wr
            wall_im_ref[rs, :] = wi
            s = CHUNK_T - 1 - d
            ss = slice(s * SSM_GROUP, (s + 1) * SSM_GROUP)
            inj_re_ref[ss, :] = pr * bbar_r - pi * bbar_i
            inj_im_ref[ss, :] = pr * bbar_i + pi * bbar_r
        if d >= 1:
            ts = slice((d - 1) * SSM_GROUP, d * SSM_GROUP)
            out_re_ref[ts, :] = wr
            out_imn_ref[ts, :] = -wi
        if d < CHUNK_T:
            pr, pi = pr * abar_r - pi * abar_i, pr * abar_i + pi * abar_r
    a_chunk_r, a_chunk_i = pr, pi

    assert LANES == 2 * SSM_STATE
    lane_g = lax.broadcasted_iota(jnp.int32, (SSM_GROUP, LANES), 1)
    lane_t = lax.broadcasted_iota(jnp.int32, (MXU_DIM, LANES), 1)
    lane_s = lax.broadcasted_iota(jnp.int32, (SUBLANES, LANES), 1)
    lane_k = lax.broadcasted_iota(jnp.int32, (SSM_GROUP, MXU_DIM), 1)
    swap = lambda v: pltpu.roll(v, SSM_STATE, axis=1)
    for k in range(GP // LANES):
        sl = slice(k * LANES, (k + 1) * LANES)
        wr_t = wall_re_ref[:, sl]
        wi_t = wall_im_ref[:, sl]
        er, ei = inj_re_ref[:, sl], inj_im_ref[:, sl]
        er_sw, ei_sw = swap(er), swap(ei)
        out_r_t = out_re_ref[:, sl].T
        out_i_t = out_imn_ref[:, sl].T
        ar8 = jnp.broadcast_to(a_chunk_r[:, sl], (SUBLANES, LANES))
        ai8 = jnp.broadcast_to(a_chunk_i[:, sl], (SUBLANES, LANES))
        for half in range(2):
            g = 2 * k + half
            lo_t = lane_t < SSM_STATE
            lo_s = lane_s < SSM_STATE
            if half == 0:
                first, second = jnp.where(lo_t, er, ei_sw), jnp.where(lo_t, ei, er_sw)
                ar_g, ai_g = jnp.where(lo_s, ar8, swap(ar8)), jnp.where(lo_s, ai8, swap(ai8))
            else:
                first, second = jnp.where(lo_t, er_sw, ei), jnp.where(lo_t, ei_sw, er)
                ar_g, ai_g = jnp.where(lo_s, swap(ar8), ar8), jnp.where(lo_s, swap(ai8), ai8)
            s4_ref[0, g, :, 0:LANES] = first.astype(_BF16)
            s4_ref[0, g, :, LANES:2 * LANES] = second.astype(_BF16)
            ps = slice(half * SSM_STATE, (half + 1) * SSM_STATE)
            yin_ref[0, g, 0:SSM_STATE, :] = out_r_t[ps].astype(_BF16)
            yin_ref[0, g, SSM_STATE:STATE2, :] = out_i_t[ps].astype(_BF16)
            apat_ref[0, g, 0] = ar_g
            apat_ref[0, g, 1] = jnp.where(lane_s < SSM_STATE, -ai_g, ai_g)
            apat_ref[0, g, 2] = jnp.where(lane_s < SSM_STATE, ai_g, -ai_g)
            mine = (lane_g < SSM_STATE) == (half == 0)
            kd = (_dot_nt(jnp.where(mine, bbar_r[:, sl], 0.0), wr_t)
                  - _dot_nt(jnp.where(mine, bbar_i[:, sl], 0.0), wi_t))
            for s in range(CHUNK_T):
                blk = kd if s == 0 else jnp.where(
                    lane_k >= s * SSM_GROUP, pltpu.roll(kd, s * SSM_GROUP, axis=1), 0.0)
                kin_ref[0, g, s * SSM_GROUP:(s + 1) * SSM_GROUP, :] = blk.astype(_BF16)


def _ssm_prep(lam_re, lam_im, log_dt, b_re, b_im, c_re, c_im):
    lam_re = lam_re.reshape(DEPTH, 1, GP)
    lam_im = lam_im.reshape(DEPTH, 1, GP)
    logdt = jnp.repeat(log_dt, SSM_STATE, axis=-1).reshape(DEPTH, 1, GP)
    bt_re = b_re.transpose(0, 3, 1, 2).reshape(DEPTH, SSM_GROUP, GP)
    bt_im = b_im.transpose(0, 3, 1, 2).reshape(DEPTH, SSM_GROUP, GP)
    ct_re = c_re.transpose(0, 2, 1, 3).reshape(DEPTH, SSM_GROUP, GP)
    ct_im = c_im.transpose(0, 2, 1, 3).reshape(DEPTH, SSM_GROUP, GP)
    vec = pl.BlockSpec((1, 1, GP), lambda l: (l, 0, 0))
    mat = pl.BlockSpec((1, SSM_GROUP, GP), lambda l: (l, 0, 0))
    per_group = lambda r, c: pl.BlockSpec((1, SSM_GROUPS, r, c), lambda l: (l, 0, 0, 0))
    return pl.pallas_call(
        _ssm_prep_kernel,
        grid=(DEPTH,),
        in_specs=[vec, vec, vec, mat, mat, mat, mat],
        out_specs=[
            pl.BlockSpec((1, 2, GP), lambda l: (l, 0, 0)),
            pl.BlockSpec((1, N_CHUNKS, MXU_DIM, 2 * CHUNK_STATE), lambda l: (l, 0, 0, 0)),
            pl.BlockSpec((1, N_CHUNKS, 2 * CHUNK_STATE, MXU_DIM), lambda l: (l, 0, 0, 0)),
            pl.BlockSpec((1, SSM_GROUPS, 3, SUBLANES, STATE2), lambda l: (l, 0, 0, 0, 0)),
            per_group(MXU_DIM, MXU_DIM), per_group(MXU_DIM, MXU_DIM), per_group(STATE2, MXU_DIM),
        ],
        out_shape=[
            jax.ShapeDtypeStruct((DEPTH, 2, GP), _F32),
            jax.ShapeDtypeStruct((DEPTH, N_CHUNKS, MXU_DIM, 2 * CHUNK_STATE), _BF16),
            jax.ShapeDtypeStruct((DEPTH, N_CHUNKS, 2 * CHUNK_STATE, MXU_DIM), _BF16),
            jax.ShapeDtypeStruct((DEPTH, SSM_GROUPS, 3, SUBLANES, STATE2), _F32),
            jax.ShapeDtypeStruct((DEPTH, SSM_GROUPS, MXU_DIM, MXU_DIM), _BF16),
            jax.ShapeDtypeStruct((DEPTH, SSM_GROUPS, MXU_DIM, MXU_DIM), _BF16),
            jax.ShapeDtypeStruct((DEPTH, SSM_GROUPS, STATE2, MXU_DIM), _BF16),
        ],
        scratch_shapes=[pltpu.VMEM((MXU_DIM, GP), _F32)] * 6,
        compiler_params=pltpu.CompilerParams(vmem_limit_bytes=VMEM_LIMIT),
        name="ssm_prep",
    )(lam_re, lam_im, logdt, bt_re, bt_im, ct_re, ct_im)


def _to_time_major(x_ref, slab_ref):
    batch, steps, width = x_ref.shape
    for n in range(batch):
        for j in range(width // LANES):
            slab_ref[j, pl.ds(n, steps, stride=batch), :] = x_ref[n, :, j * LANES:(j + 1) * LANES]
    return jnp.concatenate([slab_ref[j] for j in range(width // LANES)], axis=1)


def _from_time_major(x, slab_ref, o_ref):
    batch, steps, width = o_ref.shape
    for j in range(width // LANES):
        slab_ref[j] = x[:, j * LANES:(j + 1) * LANES]
    for n in range(batch):
        for j in range(width // LANES):
            o_ref[n, :, j * LANES:(j + 1) * LANES] = slab_ref[j, pl.ds(n, steps, stride=batch), :]


BLOCKS_PER_TILE = LANES // SSM_GROUP


def _lane_block_transpose(vs):
    vs = list(vs)
    lane = lax.broadcasted_iota(jnp.int32, vs[0].shape, 1)
    block_bits = SSM_GROUP.bit_length() - 1
    for b in reversed(range(BLOCKS_PER_TILE.bit_length() - 1)):
        d = 1 << b
        hi = ((lane >> (block_bits + b)) & 1) == 1
        for r in range(BLOCKS_PER_TILE):
            if r & d:
                continue
            lo_arr, hi_arr = vs[r], vs[r + d]
            vs[r] = jnp.where(hi, pltpu.roll(hi_arr, d * SSM_GROUP, axis=1), lo_arr)
            vs[r + d] = jnp.where(hi, hi_arr, pltpu.roll(lo_arr, LANES - d * SSM_GROUP, axis=1))
    return vs


def _to_group_major(z_ref, u_ref):
    chunks = z_ref.shape[0]
    for q in range(SSM_WIDTH // LANES):
        for half in range(CHUNK_T // BLOCKS_PER_TILE):
            vs = []
            for i in range(BLOCKS_PER_TILE):
                s = half * BLOCKS_PER_TILE + i
                v = z_ref[:, s * SUBLANES:(s + 1) * SUBLANES, q * LANES:(q + 1) * LANES]
                vs.append(v.reshape(chunks * SUBLANES, LANES))
            for gl, v in enumerate(_lane_block_transpose(vs)):
                c0 = (q * BLOCKS_PER_TILE + gl) * MXU_DIM + half * LANES
                u_ref[:, c0:c0 + LANES] = v.astype(u_ref.dtype)


def _from_group_major(y_ref, o_ref):
    chunks = o_ref.shape[0]
    for q in range(SSM_WIDTH // LANES):
        for half in range(CHUNK_T // BLOCKS_PER_TILE):
            vs = []
            for gl in range(BLOCKS_PER_TILE):
                c0 = (q * BLOCKS_PER_TILE + gl) * MXU_DIM + half * LANES
                vs.append(y_ref[:, c0:c0 + LANES])
            for i, v in enumerate(_lane_block_transpose(vs)):
                t = half * BLOCKS_PER_TILE + i
                o_ref[:, t * SUBLANES:(t + 1) * SUBLANES, q * LANES:(q + 1) * LANES] = (
                    v.reshape(chunks, SUBLANES, LANES))


def _cast_chunks(src_refs, dst_refs):
    for src, dst in zip(src_refs, dst_refs):
        dst[...] = src[...].astype(_BF16)


def _front_kernel(native_in, x_ref, g1_ref, win_ref, cw_ref, cb_ref,
                  ya_ref, zu_ref, u_ref, tail_ref, *rest):
    rows = ya_ref.shape[0]
    halo = (CONV_K - 1) * SUBLANES
    cw = CONV_WIDTH

    @pl.when(pl.program_id(0) == 0)
    def _():
        tail_ref[...] = jnp.zeros_like(tail_ref)

    if native_in:
        xtm_ref, slab_ref = rest
        x = _to_time_major(x_ref, slab_ref)
        xtm_ref[...] = x
    else:
        x = x_ref[...]
    h = _rmsnorm(x, g1_ref[...]).astype(_BF16)
    zu = _dot(h, win_ref[:, 3 * cw:3 * cw + SSM_WIDTH])
    zu_ref[...] = zu.reshape(zu_ref.shape)
    _to_group_major(zu_ref, u_ref)

    zb = _dot(h, win_ref[:, 0:cw])
    zc = _dot(h, win_ref[:, cw:2 * cw])
    zv = _dot(h, win_ref[:, 2 * cw:3 * cw])
    cin = zc * zv
    pad = jnp.concatenate([tail_ref[...], cin], axis=0)
    conv = cb_ref[...]
    for k in range(CONV_K):
        conv = conv + cw_ref[k:k + 1, :] * pad[k * SUBLANES:k * SUBLANES + rows]
    tail_ref[...] = cin[rows - halo:rows]
    ya_ref[...] = (zb * conv).astype(_BF16)


def _s5_chunk_kernel(u_ref, s4_ref, kin_ref, yin_ref, apat_ref, y_ref, st_ref, g4_ref, hin_ref):
    n_chunks = u_ref.shape[0] // SUBLANES
    for gi in range(S5_GROUPS_PER_STEP):
        cols = slice(gi * MXU_DIM, (gi + 1) * MXU_DIM)
        u = u_ref[:, cols]
        g4_ref[gi] = _dot(u, s4_ref[gi])
        a1, a2, a3 = apat_ref[gi, 0], apat_ref[gi, 1], apat_ref[gi, 2]
        p = jnp.zeros((SUBLANES, STATE2), _F32)
        q = jnp.zeros((SUBLANES, STATE2), _F32)
        for j in range(n_chunks):
            rs = pl.ds(j * SUBLANES, SUBLANES)
            hin_ref[gi, rs, :] = p
            p, q = (a1 * p + a2 * q + g4_ref[gi, rs, 0:STATE2],
                    a1 * q + a3 * p + g4_ref[gi, rs, STATE2:2 * STATE2])
        st_ref[gi] = p
        y_ref[:, cols] = (_dot(u, kin_ref[gi])
                          + _dot(hin_ref[gi].astype(_BF16), yin_ref[gi]))


N_BACK_WEIGHTS = 7


def _back_kernel(n_cast, x_ref, ya_ref, zu_ref, yg_ref, *refs):
    g1_ref, win_ref, d_ref, wglu_ref, bglu_ref, wbr_ref, wout_ref = refs[:N_BACK_WEIGHTS]
    refs = refs[N_BACK_WEIGHTS:]
    cast_src, xo_ref = refs[:n_cast], refs[n_cast]
    cast_dst, ys_ref = refs[n_cast + 1:2 * n_cast + 1], refs[2 * n_cast + 1]
    rows = xo_ref.shape[0]
    cw = CONV_WIDTH
    _cast_chunks(cast_src, cast_dst)

    x = x_ref[...]
    h = _rmsnorm(x, g1_ref[...]).astype(_BF16)
    ga = _dot(h, win_ref[:, 0:D_MODEL])
    gs = _dot(h, win_ref[:, D_MODEL:2 * D_MODEL])
    oa = _dot(ya_ref[...], wbr_ref[0:cw, :])

    _from_group_major(yg_ref, ys_ref)
    zu = zu_ref[...].reshape(rows, SSM_WIDTH)
    y = ys_ref[...].reshape(rows, SSM_WIDTH) + d_ref[...] * zu
    gy = _gelu_exact(y)
    ys = gy * jax.nn.sigmoid(_dot(gy.astype(_BF16), wglu_ref[...]) + bglu_ref[...])
    ob = _dot(ys.astype(_BF16), wbr_ref[cw:cw + SSM_WIDTH, :])
    m = jax.nn.sigmoid(ga) * oa + jax.nn.sigmoid(gs) * ob
    xo_ref[...] = x + _dot(m.astype(_BF16), wout_ref[...])


def _mixer_sample_kernel(x_ref, buf_ref, h0r_ref, h0i_ref, g1_ref, win_ref, cw_ref, cb_ref,
                         abar_ref, d_ref, wbu_ref, cmat_ref, wglu_ref, bglu_ref, wbr_ref, wout_ref,
                         xo_ref, cin_ref, str_ref, sti_ref):
    cw = CONV_WIDTH
    x = x_ref[...]
    h = _rmsnorm(x, g1_ref[...]).astype(_BF16)
    zb = _dot(h, win_ref[:, 0:cw])
    zc = _dot(h, win_ref[:, cw:2 * cw])
    zv = _dot(h, win_ref[:, 2 * cw:3 * cw])
    cin = zc * zv
    cin_ref[...] = cin
    conv = cb_ref[...]
    for k in range(CONV_K - 1):
        conv = conv + cw_ref[k:k + 1, :] * buf_ref[k]
    conv = conv + cw_ref[CONV_K - 1:CONV_K, :] * cin
    ya = (zb * conv).astype(_BF16)

    zu = _dot(h, win_ref[:, 3 * cw:3 * cw + SSM_WIDTH])
    zub = zu.astype(_BF16)
    ar = abar_ref[0:1, :]
    ai = abar_ref[1:2, :]
    h0r = h0r_ref[...]
    h0i = h0i_ref[...]
    ys = []
    for k in range(N_CHUNKS):
        sl = slice(k * CHUNK_STATE, (k + 1) * CHUNK_STATE)
        bu = _dot(zub[:, k * MXU_DIM:(k + 1) * MXU_DIM], wbu_ref[k])
        hr = ar[:, sl] * h0r[:, sl] - ai[:, sl] * h0i[:, sl] + bu[:, 0:CHUNK_STATE]
        hi = ar[:, sl] * h0i[:, sl] + ai[:, sl] * h0r[:, sl] + bu[:, CHUNK_STATE:2 * CHUNK_STATE]
        str_ref[:, sl] = hr
        sti_ref[:, sl] = hi
        ys.append(_dot(hr.astype(_BF16), cmat_ref[k, 0:CHUNK_STATE, :])
                  + _dot(hi.astype(_BF16), cmat_ref[k, CHUNK_STATE:2 * CHUNK_STATE, :]))
    y = jnp.concatenate(ys, axis=1) + d_ref[...] * zu
    gy = _gelu_exact(y)
    ys = gy * jax.nn.sigmoid(_dot(gy.astype(_BF16), wglu_ref[...]) + bglu_ref[...])

    oa = _dot(ya, wbr_ref[0:cw, :])
    ob = _dot(ys.astype(_BF16), wbr_ref[cw:cw + SSM_WIDTH, :])
    g0 = 3 * cw + SSM_WIDTH
    ga = _dot(h, win_ref[:, g0:g0 + D_MODEL])
    gs = _dot(h, win_ref[:, g0 + D_MODEL:g0 + 2 * D_MODEL])
    m = jax.nn.sigmoid(ga) * oa + jax.nn.sigmoid(gs) * ob
    xo_ref[...] = x + _dot(m.astype(_BF16), wout_ref[...])


def _resident(arr, layer):
    if arr.shape[0] == DEPTH and arr.ndim >= 3:
        shape = arr.shape[1:]
        return pl.BlockSpec((None,) + shape, lambda i: (layer,) + (0,) * len(shape),
                            pipeline_mode=pl.Buffered(1))
    return pl.BlockSpec(arr.shape, lambda i: (0,) * arr.ndim, pipeline_mode=pl.Buffered(1))


def _resident_cols(arr, half):
    rows, cols = arr.shape
    return pl.BlockSpec((rows, cols // 2), lambda i: (0, half), pipeline_mode=pl.Buffered(1))


def _cast_specs(arr, layer, n_steps):
    _, rows, cols = arr.shape
    hold = 1
    while (rows * hold) % (n_steps * BF16_SUBLANES):
        hold *= 2
    chunk = rows * hold // n_steps
    in_spec = pl.BlockSpec((None, chunk, cols), lambda i: (layer, i // hold, 0))
    out_spec = pl.BlockSpec((chunk, cols), lambda i: (i // hold, 0))
    return in_spec, out_spec, jax.ShapeDtypeStruct((rows, cols), _BF16)


def _slab_scratch(rows):
    return pltpu.VMEM((D_MODEL // LANES, rows, LANES), _F32)


def _prompt_x_spec(native_in):
    if native_in:
        return pl.BlockSpec((SUBLANES, ROW_BLOCK // SUBLANES, D_MODEL), lambda i: (0, i, 0))
    return pl.BlockSpec((ROW_BLOCK, D_MODEL), lambda i: (i, 0))


_ARBITRARY = pltpu.CompilerParams(dimension_semantics=("arbitrary",), vmem_limit_bytes=VMEM_LIMIT)


def _cast_kernel(n_cast, *refs):
    _cast_chunks(refs[:n_cast], refs[n_cast:])


def _cast_weights(arrs, layer, n_steps):
    specs = [_cast_specs(a, layer, n_steps) for a in arrs]
    return pl.pallas_call(
        functools.partial(_cast_kernel, len(arrs)),
        grid=(n_steps,),
        in_specs=[s[0] for s in specs],
        out_specs=[s[1] for s in specs],
        out_shape=[s[2] for s in specs],
        compiler_params=pltpu.CompilerParams(dimension_semantics=("arbitrary",)),
        name="cast_weights",
    )(*arrs)


def _mixer_front(layer, x, g1, w_in_b, conv_w, conv_b):
    native_in = x.ndim == 3
    rows = x.shape[0] * x.shape[1] if native_in else x.shape[0]
    chunks = rows // CHUNK_ROWS
    blk_chunks = ROW_BLOCK // CHUNK_ROWS
    halo = (CONV_K - 1) * SUBLANES
    row_spec = pl.BlockSpec((ROW_BLOCK, D_MODEL), lambda i: (i, 0))
    return pl.pallas_call(
        functools.partial(_front_kernel, native_in),
        grid=(rows // ROW_BLOCK,),
        in_specs=[_prompt_x_spec(native_in), _resident(g1, layer), _resident_cols(w_in_b, 0),
                  _resident(conv_w, layer), _resident(conv_b, layer)],
        out_specs=[
            pl.BlockSpec((ROW_BLOCK, CONV_WIDTH), lambda i: (i, 0)),
            pl.BlockSpec((blk_chunks, CHUNK_ROWS, SSM_WIDTH), lambda i: (i, 0, 0)),
            pl.BlockSpec((blk_chunks * SUBLANES, GROUP_LANES), lambda i: (i, 0)),
            pl.BlockSpec((halo, CONV_WIDTH), lambda i: (0, 0)),
        ] + ([row_spec] if native_in else []),
        out_shape=[
            jax.ShapeDtypeStruct((rows, CONV_WIDTH), _BF16),
            jax.ShapeDtypeStruct((chunks, CHUNK_ROWS, SSM_WIDTH), _F32),
            jax.ShapeDtypeStruct((chunks * SUBLANES, GROUP_LANES), _BF16),
            jax.ShapeDtypeStruct((halo, CONV_WIDTH), _F32),
        ] + ([jax.ShapeDtypeStruct((rows, D_MODEL), _F32)] if native_in else []),
        scratch_shapes=[_slab_scratch(ROW_BLOCK)] if native_in else [],
        compiler_params=_ARBITRARY,
        name="mixer_front",
    )(x, g1, w_in_b, conv_w, conv_b)


def _s5_chunked(layer, u, s4, kin, yin, apat):
    rows = u.shape[0]
    gs = S5_GROUPS_PER_STEP
    wide = pl.BlockSpec((rows, gs * MXU_DIM), lambda i: (0, i))
    table = lambda r, c: pl.BlockSpec((None, gs, r, c), lambda i: (layer, i, 0, 0))
    return pl.pallas_call(
        _s5_chunk_kernel,
        grid=(SSM_GROUPS // gs,),
        in_specs=[wide, table(MXU_DIM, MXU_DIM), table(MXU_DIM, MXU_DIM), table(STATE2, MXU_DIM),
                  pl.BlockSpec((None, gs, 3, SUBLANES, STATE2), lambda i: (layer, i, 0, 0, 0))],
        out_specs=[wide, pl.BlockSpec((gs, SUBLANES, STATE2), lambda i: (i, 0, 0))],
        out_shape=[jax.ShapeDtypeStruct((rows, GROUP_LANES), _F32),
                   jax.ShapeDtypeStruct((SSM_GROUPS, SUBLANES, STATE2), _F32)],
        scratch_shapes=[pltpu.VMEM((gs, rows, MXU_DIM), _F32), pltpu.VMEM((gs, rows, STATE2), _F32)],
        compiler_params=_ARBITRARY,
        name="s5_chunked",
    )(u, s4, kin, yin, apat)


def _mixer_back(layer, x, ya, zu, yg, weights, w_in_b, cast):
    rows = x.shape[0]
    n_steps = rows // ROW_BLOCK
    blk_chunks = ROW_BLOCK // CHUNK_ROWS
    g1, d_skip, w_glu_b, bg, w_branch_b, w_out_b = weights
    row_spec = pl.BlockSpec((ROW_BLOCK, D_MODEL), lambda i: (i, 0))
    cast_specs = [_cast_specs(a, layer, n_steps) for a in cast]
    return pl.pallas_call(
        functools.partial(_back_kernel, len(cast)),
        grid=(n_steps,),
        in_specs=([row_spec,
                   pl.BlockSpec((ROW_BLOCK, CONV_WIDTH), lambda i: (i, 0)),
                   pl.BlockSpec((blk_chunks, CHUNK_ROWS, SSM_WIDTH), lambda i: (i, 0, 0)),
                   pl.BlockSpec((blk_chunks * SUBLANES, GROUP_LANES), lambda i: (i, 0)),
                   _resident(g1, layer), _resident_cols(w_in_b, 1), _resident(d_skip, layer),
                   _resident(w_glu_b, layer), _resident(bg, layer), _resident(w_branch_b, layer),
                   _resident(w_out_b, layer)]
                  + [s[0] for s in cast_specs]),
        out_specs=[row_spec] + [s[1] for s in cast_specs],
        out_shape=[jax.ShapeDtypeStruct((rows, D_MODEL), _F32)] + [s[2] for s in cast_specs],
        scratch_shapes=[pltpu.VMEM((blk_chunks, CHUNK_ROWS, SSM_WIDTH), _F32)],
        compiler_params=_ARBITRARY,
        name="mixer_back",
    )(x, ya, zu, yg, g1, w_in_b, d_skip, w_glu_b, bg, w_branch_b, w_out_b, *cast)


def _mixer_sample(layer, x, buf, h0r, h0i, weights):
    rows = x.shape[0]
    full = lambda shape: pl.BlockSpec(shape, lambda i: (0,) * len(shape))
    state = lambda shape: pl.BlockSpec((None,) + shape, lambda i: (layer,) + (0,) * len(shape))
    return pl.pallas_call(
        _mixer_sample_kernel,
        grid=(1,),
        in_specs=[full((rows, D_MODEL)), state((CONV_K - 1, rows, CONV_WIDTH)),
                  state((rows, GP)), state((rows, GP))] + [_resident(w, layer) for w in weights],
        out_specs=[full((rows, D_MODEL)), full((rows, CONV_WIDTH)),
                   full((rows, GP)), full((rows, GP))],
        out_shape=[
            jax.ShapeDtypeStruct((rows, D_MODEL), _F32),
            jax.ShapeDtypeStruct((rows, CONV_WIDTH), _F32),
            jax.ShapeDtypeStruct((rows, GP), _F32),
            jax.ShapeDtypeStruct((rows, GP), _F32),
        ],
        compiler_params=_ARBITRARY,
        name="mixer_sample",
    )(x, buf, h0r, h0i, *weights)


_FF_CHUNKS = ((0, 1024), (1024, 1024), (2048, 768))


def _ffn_kernel(final_norm, native_out, n_cast, x_ref, g2_ref, wgu_ref, wdown_ref, gf_ref, *refs):
    cast_src, xo_ref = refs[:n_cast], refs[n_cast]
    cast_dst, (act_ref, *slab_ref) = refs[n_cast + 1:2 * n_cast + 1], refs[2 * n_cast + 1:]
    _cast_chunks(cast_src, cast_dst)
    x = x_ref[...]
    h2 = _rmsnorm(x, g2_ref[...]).astype(_BF16)
    for c0, cn in _FF_CHUNKS:
        gate = _dot(h2, wgu_ref[:, c0:c0 + cn])
        up = _dot(h2, wgu_ref[:, D_FF + c0:D_FF + c0 + cn])
        act_ref[:, c0:c0 + cn] = (gate * jax.nn.sigmoid(gate) * up).astype(_BF16)
    xn = x + _dot(act_ref[...], wdown_ref[...])
    if final_norm:
        xn = _rmsnorm(xn, gf_ref[...])
    if native_out:
        _from_time_major(xn, slab_ref[0], xo_ref)
    else:
        xo_ref[...] = xn


def _ffn(layer, x, g2, wgu, wdown, gf, final_norm, native_out, cast=()):
    rows = x.shape[0]
    rb = min(FFN_ROW_BLOCK, rows)
    cast_specs = [_cast_specs(a, layer + 1, rows // rb) for a in cast]
    row_spec = pl.BlockSpec((rb, D_MODEL), lambda i: (i, 0))
    if native_out:
        out_spec = pl.BlockSpec((SUBLANES, rb // SUBLANES, D_MODEL), lambda i: (0, i, 0))
        out_shape = jax.ShapeDtypeStruct((SUBLANES, rows // SUBLANES, D_MODEL), _F32)
    else:
        out_spec = row_spec
        out_shape = jax.ShapeDtypeStruct((rows, D_MODEL), _F32)
    outs = pl.pallas_call(
        functools.partial(_ffn_kernel, final_norm, native_out, len(cast)),
        grid=(rows // rb,),
        in_specs=([row_spec, _resident(g2, layer), _resident(wgu, layer), _resident(wdown, layer),
                   _resident(gf, 0)] + [s[0] for s in cast_specs]),
        out_specs=[out_spec] + [s[1] for s in cast_specs],
        out_shape=[out_shape] + [s[2] for s in cast_specs],
        scratch_shapes=[pltpu.VMEM((rb, D_FF), _BF16)] + ([_slab_scratch(rb)] if native_out else []),
        compiler_params=_ARBITRARY,
        name="ffn",
    )(x, g2, wgu, wdown, gf, *cast)
    return outs if cast else outs[0]


def kernel(x_prompt, x_sample, state_conv, state_ssm_re, state_ssm_im, norm1_g, w_in, conv_w,
           conv_b, ssm_lam_re, ssm_lam_im, ssm_log_dt, ssm_b_re, ssm_b_im, ssm_c_re, ssm_c_im,
           ssm_d, w_glu, b_glu, w_branch, w_out, norm2_g, w_gate_up, w_down, final_g):
    batch, seq, _ = x_prompt.shape
    dec_batch = x_sample.shape[0]
    assert batch == SUBLANES and x_sample.shape[1] == 1

    abar, wbu, cmat, apat, s4, kin, yin = _ssm_prep(
        ssm_lam_re, ssm_lam_im, ssm_log_dt, ssm_b_re, ssm_b_im, ssm_c_re, ssm_c_im)
    g1 = norm1_g.reshape(DEPTH, 1, D_MODEL)
    g2 = norm2_g.reshape(DEPTH, 1, D_MODEL)
    gf = final_g.reshape(1, D_MODEL)
    cb = conv_b.reshape(DEPTH, 1, CONV_WIDTH)
    d_skip = ssm_d.reshape(DEPTH, 1, SSM_WIDTH)
    bg = b_glu.reshape(DEPTH, 1, SSM_WIDTH)
    mixer_f32 = (w_in, w_glu, w_branch, w_out)
    ffn_f32 = (w_gate_up, w_down)
    mixer_b = _cast_weights(mixer_f32, 0, CAST_STEPS)

    xp = x_prompt
    xs = x_sample.reshape(dec_batch, D_MODEL)
    bufs = state_conv.transpose(0, 2, 1, 3)
    h0r = state_ssm_re.reshape(DEPTH, dec_batch, GP)
    h0i = state_ssm_im.reshape(DEPTH, dec_batch, GP)

    p_conv, p_re, p_im, s_conv, s_re, s_im = [], [], [], [], [], []
    for l in range(DEPTH):
        last = l == DEPTH - 1
        w_in_b, w_glu_b, w_branch_b, w_out_b = mixer_b

        ya, zu, u, tail, *x_tm = _mixer_front(l, xp, g1, w_in_b, conv_w, cb)
        if x_tm:
            xp, = x_tm
        yg, st = _s5_chunked(l, u, s4, kin, yin, apat)
        xp, w_gate_up_b, w_down_b = _mixer_back(
            l, xp, ya, zu, yg, (g1, d_skip, w_glu_b, bg, w_branch_b, w_out_b), w_in_b, ffn_f32)
        if last:
            xp = _ffn(l, xp, g2, w_gate_up_b, w_down_b, gf, True, native_out=True)
        else:
            xp, *mixer_b = _ffn(l, xp, g2, w_gate_up_b, w_down_b, gf, False, native_out=False,
                                cast=mixer_f32)
        p_conv.append(tail.reshape(CONV_K - 1, batch, CONV_WIDTH).transpose(1, 0, 2))
        p_re.append(st[:, :, :SSM_STATE].transpose(1, 0, 2))
        p_im.append(st[:, :, SSM_STATE:].transpose(1, 0, 2))

        sample_weights = (g1, w_in_b, conv_w, cb, abar, d_skip, wbu, cmat, w_glu_b, bg,
                          w_branch_b, w_out_b)
        xs, cin, ssr, ssi = _mixer_sample(l, xs, bufs, h0r, h0i, sample_weights)
        xs = _ffn(l, xs, g2, w_gate_up_b, w_down_b, gf, last, native_out=False)
        s_conv.append(jnp.concatenate([state_conv[l][:, 1:], cin[:, None, :]], axis=1))
        s_re.append(ssr.reshape(dec_batch, SSM_GROUPS, SSM_STATE))
        s_im.append(ssi.reshape(dec_batch, SSM_GROUPS, SSM_STATE))

    y_prompt = xp
    y_sample = xs.reshape(dec_batch, 1, D_MODEL)
    return (y_prompt, y_sample, jnp.stack(p_conv), jnp.stack(p_re), jnp.stack(p_im),
            jnp.stack(s_conv), jnp.stack(s_re), jnp.stack(s_im))
```

```python
import functools

import jax
import jax.numpy as jnp
from jax import lax
from jax.experimental import pallas as pl
from jax.experimental.pallas import tpu as pltpu

D_MODEL = 1024
DEPTH = 4
CONV_WIDTH = 512
CONV_K = 3
SSM_WIDTH = 512
SSM_GROUP = 16
SSM_GROUPS = 32
SSM_STATE = 64
GP = SSM_GROUPS * SSM_STATE
D_FF = 2816
N_IN = 3 * CONV_WIDTH + SSM_WIDTH + 2 * D_MODEL
RMS_EPS = 1e-6

SUBLANES = 8
LANES = 128
BF16_SUBLANES = 16
CAST_STEPS = 8
MXU_DIM = 256
GROUPS_PER_CHUNK = MXU_DIM // SSM_GROUP
N_CHUNKS = SSM_GROUPS // GROUPS_PER_CHUNK
CHUNK_STATE = GROUPS_PER_CHUNK * SSM_STATE

CHUNK_T = MXU_DIM // SSM_GROUP
CHUNK_ROWS = CHUNK_T * SUBLANES
STATE2 = 2 * SSM_STATE
GROUP_LANES = SSM_GROUPS * MXU_DIM
S5_GROUPS_PER_STEP = 4

ROW_BLOCK = 1024
FFN_ROW_BLOCK = 1024
VMEM_LIMIT = 56 * 1024 * 1024

_BF16 = jnp.bfloat16
_F32 = jnp.float32


def _dot(a, b):
    return jnp.dot(a, b, preferred_element_type=_F32)


def _dot_nt(a, b):
    return lax.dot_general(a, b, (((1,), (1,)), ((), ())), preferred_element_type=_F32)


def _rmsnorm(x, g):
    ms = jnp.mean(x * x, axis=-1, keepdims=True)
    return x * lax.rsqrt(ms + RMS_EPS) * g


def _gelu_exact(x):
    return 0.5 * x * (1.0 + lax.erf(x * (2.0 ** -0.5)))


def _ssm_prep_kernel(lam_re_ref, lam_im_ref, logdt_ref, bt_re_ref, bt_im_ref,
                     ct_re_ref, ct_im_ref,
                     abar_ref, wbu_ref, cmat_ref, apat_ref, s4_ref, kin_ref, yin_ref,
                     wall_re_ref, wall_im_ref, inj_re_ref, inj_im_ref, out_re_ref, out_imn_ref):
    lr = lam_re_ref[0]
    li = lam_im_ref[0]
    dt = jnp.exp(logdt_ref[0])
    mag = jnp.exp(lr * dt)
    ang = li * dt
    abar_r = mag * jnp.cos(ang)
    abar_i = mag * jnp.sin(ang)
    nr = abar_r - 1.0
    ni = abar_i
    den = lr * lr + li * li
    fr = (nr * lr + ni * li) / den
    fi = (ni * lr - nr * li) / den
    abar_ref[0, 0:1, :] = abar_r
    abar_ref[0, 1:2, :] = abar_i

    br = bt_re_ref[0]
    bi = bt_im_ref[0]
    bbar_r = fr * br - fi * bi
    bbar_i = fr * bi + fi * br
    ctr = ct_re_ref[0]
    cti = ct_im_ref[0]

    rows = lax.broadcasted_iota(jnp.int32, (MXU_DIM, CHUNK_STATE), 0)
    cols = lax.broadcasted_iota(jnp.int32, (MXU_DIM, CHUNK_STATE), 1)
    same_group = ((rows >> (SSM_GROUP.bit_length() - 1))
                  == (cols >> (SSM_STATE.bit_length() - 1)))

    def block_diag(src, k):
        piece = src[:, k * CHUNK_STATE:(k + 1) * CHUNK_STATE]
        tiled = jnp.concatenate([piece] * GROUPS_PER_CHUNK, axis=0)
        return jnp.where(same_group, tiled, 0.0)

    for k in range(N_CHUNKS):
        wbu_ref[0, k, :, 0:CHUNK_STATE] = block_diag(bbar_r, k).astype(_BF16)
        wbu_ref[0, k, :, CHUNK_STATE:2 * CHUNK_STATE] = block_diag(bbar_i, k).astype(_BF16)
        cmat_ref[0, k, 0:CHUNK_STATE, :] = block_diag(ctr, k).T.astype(_BF16)
        cmat_ref[0, k, CHUNK_STATE:2 * CHUNK_STATE, :] = block_diag(-cti, k).T.astype(_BF16)

    pr = jnp.ones_like(abar_r)
    pi = jnp.zeros_like(abar_r)
    for d in range(CHUNK_T + 1):
        wr = ctr * pr - cti * pi
        wi = ctr * pi + cti * pr
        if d < CHUNK_T:
            rs = slice(d * SSM_GROUP, (d + 1) * SSM_GROUP)
            wall_re_ref[rs, :] = wr
            wall_im_ref[rs, :] = wi
            s = CHUNK_T - 1 - d
            ss = slice(s * SSM_GROUP, (s + 1) * SSM_GROUP)
            inj_re_ref[ss, :] = pr * bbar_r - pi * bbar_i
            inj_im_ref[ss, :] = pr * bbar_i + pi * bbar_r
        if d >= 1:
            ts = slice((d - 1) * SSM_GROUP, d * SSM_GROUP)
            out_re_ref[ts, :] = wr
            out_imn_ref[ts, :] = -wi
        if d < CHUNK_T:
            pr, pi = pr * abar_r - pi * abar_i, pr * abar_i + pi * abar_r
    a_chunk_r, a_chunk_i = pr, pi

    assert LANES == 2 * SSM_STATE
    lane_g = lax.broadcasted_iota(jnp.int32, (SSM_GROUP, LANES), 1)
    lane_t = lax.broadcasted_iota(jnp.int32, (MXU_DIM, LANES), 1)
    lane_s = lax.broadcasted_iota(jnp.int32, (SUBLANES, LANES), 1)
    lane_k = lax.broadcasted_iota(jnp.int32, (SSM_GROUP, MXU_DIM), 1)
    swap = lambda v: pltpu.roll(v, SSM_STATE, axis=1)
    for k in range(GP // LANES):
        sl = slice(k * LANES, (k + 1) * LANES)
        wr_t = wall_re_ref[:, sl]
        wi_t = wall_im_ref[:, sl]
        er, ei = inj_re_ref[:, sl], inj_im_ref[:, sl]
        er_sw, ei_sw = swap(er), swap(ei)
        out_r_t = out_re_ref[:, sl].T
        out_i_t = out_imn_ref[:, sl].T
        ar8 = jnp.broadcast_to(a_chunk_r[:, sl], (SUBLANES, LANES))
        ai8 = jnp.broadcast_to(a_chunk_i[:, sl], (SUBLANES, LANES))
        for half in range(2):
            g = 2 * k + half
            lo_t = lane_t < SSM_STATE
            lo_s = lane_s < SSM_STATE
            if half == 0:
                first, second = jnp.where(lo_t, er, ei_sw), jnp.where(lo_t, ei, er_sw)
                ar_g, ai_g = jnp.where(lo_s, ar8, swap(ar8)), jnp.where(lo_s, ai8, swap(ai8))
            else:
                first, second = jnp.where(lo_t, er_sw, ei), jnp.where(lo_t, ei_sw, er)
                ar_g, ai_g = jnp.where(lo_s, swap(ar8), ar8), jnp.where(lo_s, swap(ai8), ai8)
            s4_ref[0, g, :, 0:LANES] = first.astype(_BF16)
            s4_ref[0, g, :, LANES:2 * LANES] = second.astype(_BF16)
            ps = slice(half * SSM_STATE, (half + 1) * SSM_STATE)
            yin_ref[0, g, 0:SSM_STATE, :] = out_r_t[ps].astype(_BF16)
            yin_ref[0, g, SSM_STATE:STATE2, :] = out_i_t[ps].astype(_BF16)
            apat_ref[0, g, 0] = ar_g
            apat_ref[0, g, 1] = jnp.where(lane_s < SSM_STATE, -ai_g, ai_g)
            apat_ref[0, g, 2] = jnp.where(lane_s < SSM_STATE, ai_g, -ai_g)
            mine = (lane_g < SSM_STATE) == (half == 0)
            kd = (_dot_nt(jnp.where(mine, bbar_r[:, sl], 0.0), wr_t)
                  - _dot_nt(jnp.where(mine, bbar_i[:, sl], 0.0), wi_t))
            for s in range(CHUNK_T):
                blk = kd if s == 0 else jnp.where(
                    lane_k >= s * SSM_GROUP, pltpu.roll(kd, s * SSM_GROUP, axis=1), 0.0)
                kin_ref[0, g, s * SSM_GROUP:(s + 1) * SSM_GROUP, :] = blk.astype(_BF16)


def _ssm_prep(lam_re, lam_im, log_dt, b_re, b_im, c_re, c_im):
    lam_re = lam_re.reshape(DEPTH, 1, GP)
    lam_im = lam_im.reshape(DEPTH, 1, GP)
    logdt = jnp.repeat(log_dt, SSM_STATE, axis=-1).reshape(DEPTH, 1, GP)
    bt_re = b_re.transpose(0, 3, 1, 2).reshape(DEPTH, SSM_GROUP, GP)
    bt_im = b_im.transpose(0, 3, 1, 2).reshape(DEPTH, SSM_GROUP, GP)
    ct_re = c_re.transpose(0, 2, 1, 3).reshape(DEPTH, SSM_GROUP, GP)
    ct_im = c_im.transpose(0, 2, 1, 3).reshape(DEPTH, SSM_GROUP, GP)
    vec = pl.BlockSpec((1, 1, GP), lambda l: (l, 0, 0))
    mat = pl.BlockSpec((1, SSM_GROUP, GP), lambda l: (l, 0, 0))
    per_group = lambda r, c: pl.BlockSpec((1, SSM_GROUPS, r, c), lambda l: (l, 0, 0, 0))
    return pl.pallas_call(
        _ssm_prep_kernel,
        grid=(DEPTH,),
        in_specs=[vec, vec, vec, mat, mat, mat, mat],
        out_specs=[
            pl.BlockSpec((1, 2, GP), lambda l: (l, 0, 0)),
            pl.BlockSpec((1, N_CHUNKS, MXU_DIM, 2 * CHUNK_STATE), lambda l: (l, 0, 0, 0)),
            pl.BlockSpec((1, N_CHUNKS, 2 * CHUNK_STATE, MXU_DIM), lambda l: (l, 0, 0, 0)),
            pl.BlockSpec((1, SSM_GROUPS, 3, SUBLANES, STATE2), lambda l: (l, 0, 0, 0, 0)),
            per_group(MXU_DIM, MXU_DIM), per_group(MXU_DIM, MXU_DIM), per_group(STATE2, MXU_DIM),
        ],
        out_shape=[
            jax.ShapeDtypeStruct((DEPTH, 2, GP), _F32),
            jax.ShapeDtypeStruct((DEPTH, N_CHUNKS, MXU_DIM, 2 * CHUNK_STATE), _BF16),
            jax.ShapeDtypeStruct((DEPTH, N_CHUNKS, 2 * CHUNK_STATE, MXU_DIM), _BF16),
            jax.ShapeDtypeStruct((DEPTH, SSM_GROUPS, 3, SUBLANES, STATE2), _F32),
            jax.ShapeDtypeStruct((DEPTH, SSM_GROUPS, MXU_DIM, MXU_DIM), _BF16),
            jax.ShapeDtypeStruct((DEPTH, SSM_GROUPS, MXU_DIM, MXU_DIM), _BF16),
            jax.ShapeDtypeStruct((DEPTH, SSM_GROUPS, STATE2, MXU_DIM), _BF16),
        ],
        scratch_shapes=[pltpu.VMEM((MXU_DIM, GP), _F32)] * 6,
        compiler_params=pltpu.CompilerParams(vmem_limit_bytes=VMEM_LIMIT),
        name="ssm_prep",
    )(lam_re, lam_im, logdt, bt_re, bt_im, ct_re, ct_im)


def _to_time_major(x_ref, slab_ref):
    batch, steps, width = x_ref.shape
    for n in range(batch):
        for j in range(width // LANES):
            slab_ref[j, pl.ds(n, steps, stride=batch), :] = x_ref[n, :, j * LANES:(j + 1) * LANES]
    return jnp.concatenate([slab_ref[j] for j in range(width // LANES)], axis=1)


def _from_time_major(x, slab_ref, o_ref):
    batch, steps, width = o_ref.shape
    for j in range(width // LANES):
        slab_ref[j] = x[:, j * LANES:(j + 1) * LANES]
    for n in range(batch):
        for j in range(width // LANES):
            o_ref[n, :, j * LANES:(j + 1) * LANES] = slab_ref[j, pl.ds(n, steps, stride=batch), :]


BLOCKS_PER_TILE = LANES // SSM_GROUP


def _lane_block_transpose(vs):
    vs = list(vs)
    lane = lax.broadcasted_iota(jnp.int32, vs[0].shape, 1)
    block_bits = SSM_GROUP.bit_length() - 1
    for b in reversed(range(BLOCKS_PER_TILE.bit_length() - 1)):
        d = 1 << b
        hi = ((lane >> (block_bits + b)) & 1) == 1
        for r in range(BLOCKS_PER_TILE):
            if r & d:
                continue
            lo_arr, hi_arr = vs[r], vs[r + d]
            vs[r] = jnp.where(hi, pltpu.roll(hi_arr, d * SSM_GROUP, axis=1), lo_arr)
            vs[r + d] = jnp.where(hi, hi_arr, pltpu.roll(lo_arr, LANES - d * SSM_GROUP, axis=1))
    return vs


def _to_group_major(z_ref, u_ref):
    chunks = z_ref.shape[0]
    for q in range(SSM_WIDTH // LANES):
        for half in range(CHUNK_T // BLOCKS_PER_TILE):
            vs = []
            for i in range(BLOCKS_PER_TILE):
                s = half * BLOCKS_PER_TILE + i
                v = z_ref[:, s * SUBLANES:(s + 1) * SUBLANES, q * LANES:(q + 1) * LANES]
                vs.append(v.reshape(chunks * SUBLANES, LANES))
            for gl, v in enumerate(_lane_block_transpose(vs)):
                c0 = (q * BLOCKS_PER_TILE + gl) * MXU_DIM + half * LANES
                u_ref[:, c0:c0 + LANES] = v.astype(u_ref.dtype)


def _from_group_major(y_ref, o_ref):
    chunks = o_ref.shape[0]
    for q in range(SSM_WIDTH // LANES):
        for half in range(CHUNK_T // BLOCKS_PER_TILE):
            vs = []
            for gl in range(BLOCKS_PER_TILE):
                c0 = (q * BLOCKS_PER_TILE + gl) * MXU_DIM + half * LANES
                vs.append(y_ref[:, c0:c0 + LANES])
            for i, v in enumerate(_lane_block_transpose(vs)):
                t = half * BLOCKS_PER_TILE + i
                o_ref[:, t * SUBLANES:(t + 1) * SUBLANES, q * LANES:(q + 1) * LANES] = (
                    v.reshape(chunks, SUBLANES, LANES))


def _cast_chunks(src_refs, dst_refs):
    for src, dst in zip(src_refs, dst_refs):
        dst[...] = src[...].astype(_BF16)


def _front_kernel(native_in, x_ref, g1_ref, win_ref, cw_ref, cb_ref,
                  ya_ref, zu_ref, u_ref, tail_ref, *rest):
    rows = ya_ref.shape[0]
    halo = (CONV_K - 1) * SUBLANES
    cw = CONV_WIDTH

    @pl.when(pl.program_id(0) == 0)
    def _():
        tail_ref[...] = jnp.zeros_like(tail_ref)

    if native_in:
        xtm_ref, slab_ref = rest
        x = _to_time_major(x_ref, slab_ref)
        xtm_ref[...] = x
    else:
        x = x_ref[...]
    h = _rmsnorm(x, g1_ref[...]).astype(_BF16)
    zu = _dot(h, win_ref[:, 3 * cw:3 * cw + SSM_WIDTH])
    zu_ref[...] = zu.reshape(zu_ref.shape)
    _to_group_major(zu_ref, u_ref)

    zb = _dot(h, win_ref[:, 0:cw])
    zc = _dot(h, win_ref[:, cw:2 * cw])
    zv = _dot(h, win_ref[:, 2 * cw:3 * cw])
    cin = zc * zv
    pad = jnp.concatenate([tail_ref[...], cin], axis=0)
    conv = cb_ref[...]
    for k in range(CONV_K):
        conv = conv + cw_ref[k:k + 1, :] * pad[k * SUBLANES:k * SUBLANES + rows]
    tail_ref[...] = cin[rows - halo:rows]
    ya_ref[...] = (zb * conv).astype(_BF16)


def _s5_chunk_kernel(u_ref, s4_ref, kin_ref, yin_ref, apat_ref, y_ref, st_ref, g4_ref, hin_ref):
    n_chunks = u_ref.shape[0] // SUBLANES
    for gi in range(S5_GROUPS_PER_STEP):
        cols = slice(gi * MXU_DIM, (gi + 1) * MXU_DIM)
        u = u_ref[:, cols]
        g4_ref[gi] = _dot(u, s4_ref[gi])
        a1, a2, a3 = apat_ref[gi, 0], apat_ref[gi, 1], apat_ref[gi, 2]
        p = jnp.zeros((SUBLANES, STATE2), _F32)
        q = jnp.zeros((SUBLANES, STATE2), _F32)
        for j in range(n_chunks):
            rs = pl.ds(j * SUBLANES, SUBLANES)
            hin_ref[gi, rs, :] = p
            p, q = (a1 * p + a2 * q + g4_ref[gi, rs, 0:STATE2],
                    a1 * q + a3 * p + g4_ref[gi, rs, STATE2:2 * STATE2])
        st_ref[gi] = p
        y_ref[:, cols] = (_dot(u, kin_ref[gi])
                          + _dot(hin_ref[gi].astype(_BF16), yin_ref[gi]))


N_BACK_WEIGHTS = 7


def _back_kernel(n_cast, x_ref, ya_ref, zu_ref, yg_ref, *refs):
    g1_ref, win_ref, d_ref, wglu_ref, bglu_ref, wbr_ref, wout_ref = refs[:N_BACK_WEIGHTS]
    refs = refs[N_BACK_WEIGHTS:]
    cast_src, xo_ref = refs[:n_cast], refs[n_cast]
    cast_dst, ys_ref = refs[n_cast + 1:2 * n_cast + 1], refs[2 * n_cast + 1]
    rows = xo_ref.shape[0]
    cw = CONV_WIDTH
    _cast_chunks(cast_src, cast_dst)

    x = x_ref[...]
    h = _rmsnorm(x, g1_ref[...]).astype(_BF16)
    ga = _dot(h, win_ref[:, 0:D_MODEL])
    gs = _dot(h, win_ref[:, D_MODEL:2 * D_MODEL])
    oa = _dot(ya_ref[...], wbr_ref[0:cw, :])

    _from_group_major(yg_ref, ys_ref)
    zu = zu_ref[...].reshape(rows, SSM_WIDTH)
    y = ys_ref[...].reshape(rows, SSM_WIDTH) + d_ref[...] * zu
    gy = _gelu_exact(y)
    ys = gy * jax.nn.sigmoid(_dot(gy.astype(_BF16), wglu_ref[...]) + bglu_ref[...])
    ob = _dot(ys.astype(_BF16), wbr_ref[cw:cw + SSM_WIDTH, :])
    m = jax.nn.sigmoid(ga) * oa + jax.nn.sigmoid(gs) * ob
    xo_ref[...] = x + _dot(m.astype(_BF16), wout_ref[...])


def _sample_layer_kernel(final_norm, x_ref, buf_ref, h0r_ref, h0i_ref, g1_ref, win_ref, cw_ref,
                         cb_ref, abar_ref, d_ref, wbu_ref, cmat_ref, wglu_ref, bglu_ref, wbr_ref,
                         wout_ref, g2_ref, wgu_ref, wdown_ref, gf_ref,
                         xo_ref, cin_ref, str_ref, sti_ref):
    cw = CONV_WIDTH
    x = x_ref[...]
    h = _rmsnorm(x, g1_ref[...]).astype(_BF16)
    zb = _dot(h, win_ref[:, 0:cw])
    zc = _dot(h, win_ref[:, cw:2 * cw])
    zv = _dot(h, win_ref[:, 2 * cw:3 * cw])
    cin = zc * zv
    cin_ref[...] = cin
    conv = cb_ref[...]
    for k in range(CONV_K - 1):
        conv = conv + cw_ref[k:k + 1, :] * buf_ref[k]
    conv = conv + cw_ref[CONV_K - 1:CONV_K, :] * cin
    ya = (zb * conv).astype(_BF16)

    zu = _dot(h, win_ref[:, 3 * cw:3 * cw + SSM_WIDTH])
    zub = zu.astype(_BF16)
    ar = abar_ref[0:1, :]
    ai = abar_ref[1:2, :]
    h0r = h0r_ref[...]
    h0i = h0i_ref[...]
    ys = []
    for k in range(N_CHUNKS):
        sl = slice(k * CHUNK_STATE, (k + 1) * CHUNK_STATE)
        bu = _dot(zub[:, k * MXU_DIM:(k + 1) * MXU_DIM], wbu_ref[k])
        hr = ar[:, sl] * h0r[:, sl] - ai[:, sl] * h0i[:, sl] + bu[:, 0:CHUNK_STATE]
        hi = ar[:, sl] * h0i[:, sl] + ai[:, sl] * h0r[:, sl] + bu[:, CHUNK_STATE:2 * CHUNK_STATE]
        str_ref[:, sl] = hr
        sti_ref[:, sl] = hi
        ys.append(_dot(hr.astype(_BF16), cmat_ref[k, 0:CHUNK_STATE, :])
                  + _dot(hi.astype(_BF16), cmat_ref[k, CHUNK_STATE:2 * CHUNK_STATE, :]))
    y = jnp.concatenate(ys, axis=1) + d_ref[...] * zu
    gy = _gelu_exact(y)
    ys = gy * jax.nn.sigmoid(_dot(gy.astype(_BF16), wglu_ref[...]) + bglu_ref[...])

    oa = _dot(ya, wbr_ref[0:cw, :])
    ob = _dot(ys.astype(_BF16), wbr_ref[cw:cw + SSM_WIDTH, :])
    g0 = 3 * cw + SSM_WIDTH
    ga = _dot(h, win_ref[:, g0:g0 + D_MODEL])
    gs = _dot(h, win_ref[:, g0 + D_MODEL:g0 + 2 * D_MODEL])
    m = jax.nn.sigmoid(ga) * oa + jax.nn.sigmoid(gs) * ob
    x = x + _dot(m.astype(_BF16), wout_ref[...])

    h2 = _rmsnorm(x, g2_ref[...]).astype(_BF16)
    acts = []
    for c0, cn in _FF_CHUNKS:
        gate = _dot(h2, wgu_ref[:, c0:c0 + cn])
        up = _dot(h2, wgu_ref[:, D_FF + c0:D_FF + c0 + cn])
        acts.append((gate * jax.nn.sigmoid(gate) * up).astype(_BF16))
    x = x + _dot(jnp.concatenate(acts, axis=1), wdown_ref[...])
    xo_ref[...] = _rmsnorm(x, gf_ref[...]) if final_norm else x


def _resident(arr, layer):
    if arr.shape[0] == DEPTH and arr.ndim >= 3:
        shape = arr.shape[1:]
        return pl.BlockSpec((None,) + shape, lambda i: (layer,) + (0,) * len(shape),
                            pipeline_mode=pl.Buffered(1))
    return pl.BlockSpec(arr.shape, lambda i: (0,) * arr.ndim, pipeline_mode=pl.Buffered(1))


def _resident_cols(arr, half):
    rows, cols = arr.shape
    return pl.BlockSpec((rows, cols // 2), lambda i: (0, half), pipeline_mode=pl.Buffered(1))


def _cast_specs(arr, layer, n_steps):
    _, rows, cols = arr.shape
    hold = 1
    while (rows * hold) % (n_steps * BF16_SUBLANES):
        hold *= 2
    chunk = rows * hold // n_steps
    in_spec = pl.BlockSpec((None, chunk, cols), lambda i: (layer, i // hold, 0))
    out_spec = pl.BlockSpec((chunk, cols), lambda i: (i // hold, 0))
    return in_spec, out_spec, jax.ShapeDtypeStruct((rows, cols), _BF16)


def _slab_scratch(rows):
    return pltpu.VMEM((D_MODEL // LANES, rows, LANES), _F32)


def _prompt_x_spec(native_in):
    if native_in:
        return pl.BlockSpec((SUBLANES, ROW_BLOCK // SUBLANES, D_MODEL), lambda i: (0, i, 0))
    return pl.BlockSpec((ROW_BLOCK, D_MODEL), lambda i: (i, 0))


_ARBITRARY = pltpu.CompilerParams(dimension_semantics=("arbitrary",), vmem_limit_bytes=VMEM_LIMIT)


def _cast_kernel(n_cast, *refs):
    _cast_chunks(refs[:n_cast], refs[n_cast:])


def _cast_weights(arrs, layer, n_steps):
    specs = [_cast_specs(a, layer, n_steps) for a in arrs]
    return pl.pallas_call(
        functools.partial(_cast_kernel, len(arrs)),
        grid=(n_steps,),
        in_specs=[s[0] for s in specs],
        out_specs=[s[1] for s in specs],
        out_shape=[s[2] for s in specs],
        compiler_params=pltpu.CompilerParams(dimension_semantics=("arbitrary",)),
        name="cast_weights",
    )(*arrs)


def _mixer_front(layer, x, g1, w_in_b, conv_w, conv_b):
    native_in = x.ndim == 3
    rows = x.shape[0] * x.shape[1] if native_in else x.shape[0]
    chunks = rows // CHUNK_ROWS
    blk_chunks = ROW_BLOCK // CHUNK_ROWS
    halo = (CONV_K - 1) * SUBLANES
    row_spec = pl.BlockSpec((ROW_BLOCK, D_MODEL), lambda i: (i, 0))
    return pl.pallas_call(
        functools.partial(_front_kernel, native_in),
        grid=(rows // ROW_BLOCK,),
        in_specs=[_prompt_x_spec(native_in), _resident(g1, layer), _resident_cols(w_in_b, 0),
                  _resident(conv_w, layer), _resident(conv_b, layer)],
        out_specs=[
            pl.BlockSpec((ROW_BLOCK, CONV_WIDTH), lambda i: (i, 0)),
            pl.BlockSpec((blk_chunks, CHUNK_ROWS, SSM_WIDTH), lambda i: (i, 0, 0)),
            pl.BlockSpec((blk_chunks * SUBLANES, GROUP_LANES), lambda i: (i, 0)),
            pl.BlockSpec((halo, CONV_WIDTH), lambda i: (0, 0)),
        ] + ([row_spec] if native_in else []),
        out_shape=[
            jax.ShapeDtypeStruct((rows, CONV_WIDTH), _BF16),
            jax.ShapeDtypeStruct((chunks, CHUNK_ROWS, SSM_WIDTH), _F32),
            jax.ShapeDtypeStruct((chunks * SUBLANES, GROUP_LANES), _BF16),
            jax.ShapeDtypeStruct((halo, CONV_WIDTH), _F32),
        ] + ([jax.ShapeDtypeStruct((rows, D_MODEL), _F32)] if native_in else []),
        scratch_shapes=[_slab_scratch(ROW_BLOCK)] if native_in else [],
        compiler_params=_ARBITRARY,
        name="mixer_front",
    )(x, g1, w_in_b, conv_w, conv_b)


def _s5_chunked(layer, u, s4, kin, yin, apat):
    rows = u.shape[0]
    gs = S5_GROUPS_PER_STEP
    wide = pl.BlockSpec((rows, gs * MXU_DIM), lambda i: (0, i))
    table = lambda r, c: pl.BlockSpec((None, gs, r, c), lambda i: (layer, i, 0, 0))
    return pl.pallas_call(
        _s5_chunk_kernel,
        grid=(SSM_GROUPS // gs,),
        in_specs=[wide, table(MXU_DIM, MXU_DIM), table(MXU_DIM, MXU_DIM), table(STATE2, MXU_DIM),
                  pl.BlockSpec((None, gs, 3, SUBLANES, STATE2), lambda i: (layer, i, 0, 0, 0))],
        out_specs=[wide, pl.BlockSpec((gs, SUBLANES, STATE2), lambda i: (i, 0, 0))],
        out_shape=[jax.ShapeDtypeStruct((rows, GROUP_LANES), _F32),
                   jax.ShapeDtypeStruct((SSM_GROUPS, SUBLANES, STATE2), _F32)],
        scratch_shapes=[pltpu.VMEM((gs, rows, MXU_DIM), _F32), pltpu.VMEM((gs, rows, STATE2), _F32)],
        compiler_params=_ARBITRARY,
        name="s5_chunked",
    )(u, s4, kin, yin, apat)


def _mixer_back(layer, x, ya, zu, yg, weights, w_in_b, cast):
    rows = x.shape[0]
    n_steps = rows // ROW_BLOCK
    blk_chunks = ROW_BLOCK // CHUNK_ROWS
    g1, d_skip, w_glu_b, bg, w_branch_b, w_out_b = weights
    row_spec = pl.BlockSpec((ROW_BLOCK, D_MODEL), lambda i: (i, 0))
    cast_specs = [_cast_specs(a, layer, n_steps) for a in cast]
    return pl.pallas_call(
        functools.partial(_back_kernel, len(cast)),
        grid=(n_steps,),
        in_specs=([row_spec,
                   pl.BlockSpec((ROW_BLOCK, CONV_WIDTH), lambda i: (i, 0)),
                   pl.BlockSpec((blk_chunks, CHUNK_ROWS, SSM_WIDTH), lambda i: (i, 0, 0)),
                   pl.BlockSpec((blk_chunks * SUBLANES, GROUP_LANES), lambda i: (i, 0)),
                   _resident(g1, layer), _resident_cols(w_in_b, 1), _resident(d_skip, layer),
                   _resident(w_glu_b, layer), _resident(bg, layer), _resident(w_branch_b, layer),
                   _resident(w_out_b, layer)]
                  + [s[0] for s in cast_specs]),
        out_specs=[row_spec] + [s[1] for s in cast_specs],
        out_shape=[jax.ShapeDtypeStruct((rows, D_MODEL), _F32)] + [s[2] for s in cast_specs],
        scratch_shapes=[pltpu.VMEM((blk_chunks, CHUNK_ROWS, SSM_WIDTH), _F32)],
        compiler_params=_ARBITRARY,
        name="mixer_back",
    )(x, ya, zu, yg, g1, w_in_b, d_skip, w_glu_b, bg, w_branch_b, w_out_b, *cast)


def _sample_layer(layer, x, buf, h0r, h0i, weights, g2, wgu, wdown, gf, final_norm):
    rows = x.shape[0]
    full = lambda shape: pl.BlockSpec(shape, lambda i: (0,) * len(shape))
    state = lambda shape: pl.BlockSpec((None,) + shape, lambda i: (layer,) + (0,) * len(shape))
    weights = tuple(weights) + (g2, wgu, wdown)
    return pl.pallas_call(
        functools.partial(_sample_layer_kernel, final_norm),
        grid=(1,),
        in_specs=([full((rows, D_MODEL)), state((CONV_K - 1, rows, CONV_WIDTH)),
                   state((rows, GP)), state((rows, GP))] + [_resident(w, layer) for w in weights]
                  + [_resident(gf, 0)]),
        out_specs=[full((rows, D_MODEL)), full((rows, CONV_WIDTH)),
                   full((rows, GP)), full((rows, GP))],
        out_shape=[
            jax.ShapeDtypeStruct((rows, D_MODEL), _F32),
            jax.ShapeDtypeStruct((rows, CONV_WIDTH), _F32),
            jax.ShapeDtypeStruct((rows, GP), _F32),
            jax.ShapeDtypeStruct((rows, GP), _F32),
        ],
        compiler_params=_ARBITRARY,
        name="sample_layer",
    )(x, buf, h0r, h0i, *weights, gf)


_FF_CHUNKS = ((0, 1024), (1024, 1024), (2048, 768))


def _ffn_kernel(final_norm, native_out, n_cast, x_ref, g2_ref, wgu_ref, wdown_ref, gf_ref, *refs):
    cast_src, xo_ref = refs[:n_cast], refs[n_cast]
    cast_dst, (act_ref, *slab_ref) = refs[n_cast + 1:2 * n_cast + 1], refs[2 * n_cast + 1:]
    _cast_chunks(cast_src, cast_dst)
    x = x_ref[...]
    h2 = _rmsnorm(x, g2_ref[...]).astype(_BF16)
    for c0, cn in _FF_CHUNKS:
        gate = _dot(h2, wgu_ref[:, c0:c0 + cn])
        up = _dot(h2, wgu_ref[:, D_FF + c0:D_FF + c0 + cn])
        act_ref[:, c0:c0 + cn] = (gate * jax.nn.sigmoid(gate) * up).astype(_BF16)
    xn = x + _dot(act_ref[...], wdown_ref[...])
    if final_norm:
        xn = _rmsnorm(xn, gf_ref[...])
    if native_out:
        _from_time_major(xn, slab_ref[0], xo_ref)
    else:
        xo_ref[...] = xn


def _ffn(layer, x, g2, wgu, wdown, gf, final_norm, native_out, cast=()):
    rows = x.shape[0]
    rb = min(FFN_ROW_BLOCK, rows)
    cast_specs = [_cast_specs(a, layer + 1, rows // rb) for a in cast]
    row_spec = pl.BlockSpec((rb, D_MODEL), lambda i: (i, 0))
    if native_out:
        out_spec = pl.BlockSpec((SUBLANES, rb // SUBLANES, D_MODEL), lambda i: (0, i, 0))
        out_shape = jax.ShapeDtypeStruct((SUBLANES, rows // SUBLANES, D_MODEL), _F32)
    else:
        out_spec = row_spec
        out_shape = jax.ShapeDtypeStruct((rows, D_MODEL), _F32)
    outs = pl.pallas_call(
        functools.partial(_ffn_kernel, final_norm, native_out, len(cast)),
        grid=(rows // rb,),
        in_specs=([row_spec, _resident(g2, layer), _resident(wgu, layer), _resident(wdown, layer),
                   _resident(gf, 0)] + [s[0] for s in cast_specs]),
        out_specs=[out_spec] + [s[1] for s in cast_specs],
        out_shape=[out_shape] + [s[2] for s in cast_specs],
        scratch_shapes=[pltpu.VMEM((rb, D_FF), _BF16)] + ([_slab_scratch(rb)] if native_out else []),
        compiler_params=_ARBITRARY,
        name="ffn",
    )(x, g2, wgu, wdown, gf, *cast)
    return outs if cast else outs[0]


def kernel(x_prompt, x_sample, state_conv, state_ssm_re, state_ssm_im, norm1_g, w_in, conv_w,
           conv_b, ssm_lam_re, ssm_lam_im, ssm_log_dt, ssm_b_re, ssm_b_im, ssm_c_re, ssm_c_im,
           ssm_d, w_glu, b_glu, w_branch, w_out, norm2_g, w_gate_up, w_down, final_g):
    batch, seq, _ = x_prompt.shape
    dec_batch = x_sample.shape[0]
    assert batch == SUBLANES and x_sample.shape[1] == 1

    abar, wbu, cmat, apat, s4, kin, yin = _ssm_prep(
        ssm_lam_re, ssm_lam_im, ssm_log_dt, ssm_b_re, ssm_b_im, ssm_c_re, ssm_c_im)
    g1 = norm1_g.reshape(DEPTH, 1, D_MODEL)
    g2 = norm2_g.reshape(DEPTH, 1, D_MODEL)
    gf = final_g.reshape(1, D_MODEL)
    cb = conv_b.reshape(DEPTH, 1, CONV_WIDTH)
    d_skip = ssm_d.reshape(DEPTH, 1, SSM_WIDTH)
    bg = b_glu.reshape(DEPTH, 1, SSM_WIDTH)
    mixer_f32 = (w_in, w_glu, w_branch, w_out)
    ffn_f32 = (w_gate_up, w_down)
    mixer_b = _cast_weights(mixer_f32, 0, CAST_STEPS)

    xp = x_prompt
    xs = x_sample.reshape(dec_batch, D_MODEL)
    bufs = state_conv.transpose(0, 2, 1, 3)
    h0r = state_ssm_re.reshape(DEPTH, dec_batch, GP)
    h0i = state_ssm_im.reshape(DEPTH, dec_batch, GP)

    p_conv, p_st, s_cin, s_re, s_im = [], [], [], [], []
    for l in range(DEPTH):
        last = l == DEPTH - 1
        w_in_b, w_glu_b, w_branch_b, w_out_b = mixer_b

        ya, zu, u, tail, *x_tm = _mixer_front(l, xp, g1, w_in_b, conv_w, cb)
        if x_tm:
            xp, = x_tm
        yg, st = _s5_chunked(l, u, s4, kin, yin, apat)
        xp, w_gate_up_b, w_down_b = _mixer_back(
            l, xp, ya, zu, yg, (g1, d_skip, w_glu_b, bg, w_branch_b, w_out_b), w_in_b, ffn_f32)
        if last:
            xp = _ffn(l, xp, g2, w_gate_up_b, w_down_b, gf, True, native_out=True)
        else:
            xp, *mixer_b = _ffn(l, xp, g2, w_gate_up_b, w_down_b, gf, False, native_out=False,
                                cast=mixer_f32)
        p_conv.append(tail)
        p_st.append(st)

        sample_weights = (g1, w_in_b, conv_w, cb, abar, d_skip, wbu, cmat, w_glu_b, bg,
                          w_branch_b, w_out_b)
        xs, cin, ssr, ssi = _sample_layer(l, xs, bufs, h0r, h0i, sample_weights,
                                          g2, w_gate_up_b, w_down_b, gf, last)
        s_cin.append(cin)
        s_re.append(ssr)
        s_im.append(ssi)

    y_prompt = xp
    y_sample = xs.reshape(dec_batch, 1, D_MODEL)
    prompt_conv = (jnp.stack(p_conv).reshape(DEPTH, CONV_K - 1, batch, CONV_WIDTH)
                   .transpose(0, 2, 1, 3))
    st_all = jnp.stack(p_st).transpose(0, 2, 1, 3)
    sample_conv = jnp.concatenate([state_conv[:, :, 1:], jnp.stack(s_cin)[:, :, None, :]], axis=2)
    group_shape = (DEPTH, dec_batch, SSM_GROUPS, SSM_STATE)
    return (y_prompt, y_sample, prompt_conv, st_all[..., :SSM_STATE], st_all[..., SSM_STATE:],
            sample_conv, jnp.stack(s_re).reshape(group_shape), jnp.stack(s_im).reshape(group_shape))
```

```python
import functools

import jax
import jax.numpy as jnp
from jax import lax
from jax.experimental import pallas as pl
from jax.experimental.pallas import tpu as pltpu

D_MODEL = 1024
DEPTH = 4
CONV_WIDTH = 512
CONV_K = 3
SSM_WIDTH = 512
SSM_GROUP = 16
SSM_GROUPS = 32
SSM_STATE = 64
GP = SSM_GROUPS * SSM_STATE
D_FF = 2816
N_IN = 3 * CONV_WIDTH + SSM_WIDTH + 2 * D_MODEL
RMS_EPS = 1e-6

SUBLANES = 8
LANES = 128
BF16_SUBLANES = 16
CAST_STEPS = 8
MXU_DIM = 256
GROUPS_PER_CHUNK = MXU_DIM // SSM_GROUP
N_CHUNKS = SSM_GROUPS // GROUPS_PER_CHUNK
CHUNK_STATE = GROUPS_PER_CHUNK * SSM_STATE

CHUNK_T = MXU_DIM // SSM_GROUP
CHUNK_ROWS = CHUNK_T * SUBLANES
STATE2 = 2 * SSM_STATE
GROUP_LANES = SSM_GROUPS * MXU_DIM
S5_GROUPS_PER_STEP = 4

ROW_BLOCK = 1024
FFN_ROW_BLOCK = 1024
VMEM_LIMIT = 56 * 1024 * 1024

_BF16 = jnp.bfloat16
_F32 = jnp.float32


def _dot(a, b):
    return jnp.dot(a, b, preferred_element_type=_F32)


def _dot_nt(a, b):
    return lax.dot_general(a, b, (((1,), (1,)), ((), ())), preferred_element_type=_F32)


def _rmsnorm(x, g):
    ms = jnp.mean(x * x, axis=-1, keepdims=True)
    return x * lax.rsqrt(ms + RMS_EPS) * g


def _gelu_exact(x):
    return 0.5 * x * (1.0 + lax.erf(x * (2.0 ** -0.5)))


def _ssm_prep_kernel(lam_re_ref, lam_im_ref, logdt_ref, bt_re_ref, bt_im_ref,
                     ct_re_ref, ct_im_ref,
                     abar_ref, wbu_ref, cmat_ref, apat_ref, s4_ref, kin_ref, yin_ref,
                     wall_re_ref, wall_im_ref, inj_re_ref, inj_im_ref, out_re_ref, out_imn_ref):
    lr = lam_re_ref[0]
    li = lam_im_ref[0]
    dt = jnp.exp(logdt_ref[0])
    mag = jnp.exp(lr * dt)
    ang = li * dt
    abar_r = mag * jnp.cos(ang)
    abar_i = mag * jnp.sin(ang)
    nr = abar_r - 1.0
    ni = abar_i
    den = lr * lr + li * li
    fr = (nr * lr + ni * li) / den
    fi = (ni * lr - nr * li) / den
    abar_ref[0, 0:1, :] = abar_r
    abar_ref[0, 1:2, :] = abar_i

    br = bt_re_ref[0]
    bi = bt_im_ref[0]
    bbar_r = fr * br - fi * bi
    bbar_i = fr * bi + fi * br
    ctr = ct_re_ref[0]
    cti = ct_im_ref[0]

    rows = lax.broadcasted_iota(jnp.int32, (MXU_DIM, CHUNK_STATE), 0)
    cols = lax.broadcasted_iota(jnp.int32, (MXU_DIM, CHUNK_STATE), 1)
    same_group = ((rows >> (SSM_GROUP.bit_length() - 1))
                  == (cols >> (SSM_STATE.bit_length() - 1)))

    def block_diag(src, k):
        piece = src[:, k * CHUNK_STATE:(k + 1) * CHUNK_STATE]
        tiled = jnp.concatenate([piece] * GROUPS_PER_CHUNK, axis=0)
        return jnp.where(same_group, tiled, 0.0)

    for k in range(N_CHUNKS):
        wbu_ref[0, k, :, 0:CHUNK_STATE] = block_diag(bbar_r, k).astype(_BF16)
        wbu_ref[0, k, :, CHUNK_STATE:2 * CHUNK_STATE] = block_diag(bbar_i, k).astype(_BF16)
        cmat_ref[0, k, 0:CHUNK_STATE, :] = block_diag(ctr, k).T.astype(_BF16)
        cmat_ref[0, k, CHUNK_STATE:2 * CHUNK_STATE, :] = block_diag(-cti, k).T.astype(_BF16)

    pr = jnp.ones_like(abar_r)
    pi = jnp.zeros_like(abar_r)
    for d in range(CHUNK_T + 1):
        wr = ctr * pr - cti * pi
        wi = ctr * pi + cti * pr
        if d < CHUNK_T:
            rs = slice(d * SSM_GROUP, (d + 1) * SSM_GROUP)
            wall_re_ref[rs, :] = wr
            wall_im_ref[rs, :] = wi
            s = CHUNK_T - 1 - d
            ss = slice(s * SSM_GROUP, (s + 1) * SSM_GROUP)
            inj_re_ref[ss, :] = pr * bbar_r - pi * bbar_i
            inj_im_ref[ss, :] = pr * bbar_i + pi * bbar_r
        if d >= 1:
            ts = slice((d - 1) * SSM_GROUP, d * SSM_GROUP)
            out_re_ref[ts, :] = wr
            out_imn_ref[ts, :] = -wi
        if d < CHUNK_T:
            pr, pi = pr * abar_r - pi * abar_i, pr * abar_i + pi * abar_r
    a_chunk_r, a_chunk_i = pr, pi

    assert LANES == 2 * SSM_STATE
    lane_g = lax.broadcasted_iota(jnp.int32, (SSM_GROUP, LANES), 1)
    lane_t = lax.broadcasted_iota(jnp.int32, (MXU_DIM, LANES), 1)
    lane_s = lax.broadcasted_iota(jnp.int32, (SUBLANES, LANES), 1)
    lane_k = lax.broadcasted_iota(jnp.int32, (SSM_GROUP, MXU_DIM), 1)
    swap = lambda v: pltpu.roll(v, SSM_STATE, axis=1)
    for k in range(GP // LANES):
        sl = slice(k * LANES, (k + 1) * LANES)
        wr_t = wall_re_ref[:, sl]
        wi_t = wall_im_ref[:, sl]
        er, ei = inj_re_ref[:, sl], inj_im_ref[:, sl]
        er_sw, ei_sw = swap(er), swap(ei)
        out_r_t = out_re_ref[:, sl].T
        out_i_t = out_imn_ref[:, sl].T
        ar8 = jnp.broadcast_to(a_chunk_r[:, sl], (SUBLANES, LANES))
        ai8 = jnp.broadcast_to(a_chunk_i[:, sl], (SUBLANES, LANES))
        for half in range(2):
            g = 2 * k + half
            lo_t = lane_t < SSM_STATE
            lo_s = lane_s < SSM_STATE
            if half == 0:
                first, second = jnp.where(lo_t, er, ei_sw), jnp.where(lo_t, ei, er_sw)
                ar_g, ai_g = jnp.where(lo_s, ar8, swap(ar8)), jnp.where(lo_s, ai8, swap(ai8))
            else:
                first, second = jnp.where(lo_t, er_sw, ei), jnp.where(lo_t, ei_sw, er)
                ar_g, ai_g = jnp.where(lo_s, swap(ar8), ar8), jnp.where(lo_s, swap(ai8), ai8)
            s4_ref[0, g, :, 0:LANES] = first.astype(_BF16)
            s4_ref[0, g, :, LANES:2 * LANES] = second.astype(_BF16)
            ps = slice(half * SSM_STATE, (half + 1) * SSM_STATE)
            yin_ref[0, g, 0:SSM_STATE, :] = out_r_t[ps].astype(_BF16)
            yin_ref[0, g, SSM_STATE:STATE2, :] = out_i_t[ps].astype(_BF16)
            apat_ref[0, g, 0] = ar_g
            apat_ref[0, g, 1] = jnp.where(lane_s < SSM_STATE, -ai_g, ai_g)
            apat_ref[0, g, 2] = jnp.where(lane_s < SSM_STATE, ai_g, -ai_g)
            mine = (lane_g < SSM_STATE) == (half == 0)
            kd = (_dot_nt(jnp.where(mine, bbar_r[:, sl], 0.0), wr_t)
                  - _dot_nt(jnp.where(mine, bbar_i[:, sl], 0.0), wi_t))
            for s in range(CHUNK_T):
                blk = kd if s == 0 else jnp.where(
                    lane_k >= s * SSM_GROUP, pltpu.roll(kd, s * SSM_GROUP, axis=1), 0.0)
                kin_ref[0, g, s * SSM_GROUP:(s + 1) * SSM_GROUP, :] = blk.astype(_BF16)


def _ssm_prep(lam_re, lam_im, log_dt, b_re, b_im, c_re, c_im):
    lam_re = lam_re.reshape(DEPTH, 1, GP)
    lam_im = lam_im.reshape(DEPTH, 1, GP)
    logdt = jnp.repeat(log_dt, SSM_STATE, axis=-1).reshape(DEPTH, 1, GP)
    bt_re = b_re.transpose(0, 3, 1, 2).reshape(DEPTH, SSM_GROUP, GP)
    bt_im = b_im.transpose(0, 3, 1, 2).reshape(DEPTH, SSM_GROUP, GP)
    ct_re = c_re.transpose(0, 2, 1, 3).reshape(DEPTH, SSM_GROUP, GP)
    ct_im = c_im.transpose(0, 2, 1, 3).reshape(DEPTH, SSM_GROUP, GP)
    vec = pl.BlockSpec((1, 1, GP), lambda l: (l, 0, 0))
    mat = pl.BlockSpec((1, SSM_GROUP, GP), lambda l: (l, 0, 0))
    per_group = lambda r, c: pl.BlockSpec((1, SSM_GROUPS, r, c), lambda l: (l, 0, 0, 0))
    return pl.pallas_call(
        _ssm_prep_kernel,
        grid=(DEPTH,),
        in_specs=[vec, vec, vec, mat, mat, mat, mat],
        out_specs=[
            pl.BlockSpec((1, 2, GP), lambda l: (l, 0, 0)),
            pl.BlockSpec((1, N_CHUNKS, MXU_DIM, 2 * CHUNK_STATE), lambda l: (l, 0, 0, 0)),
            pl.BlockSpec((1, N_CHUNKS, 2 * CHUNK_STATE, MXU_DIM), lambda l: (l, 0, 0, 0)),
            pl.BlockSpec((1, SSM_GROUPS, 3, SUBLANES, STATE2), lambda l: (l, 0, 0, 0, 0)),
            per_group(MXU_DIM, MXU_DIM), per_group(MXU_DIM, MXU_DIM), per_group(STATE2, MXU_DIM),
        ],
        out_shape=[
            jax.ShapeDtypeStruct((DEPTH, 2, GP), _F32),
            jax.ShapeDtypeStruct((DEPTH, N_CHUNKS, MXU_DIM, 2 * CHUNK_STATE), _BF16),
            jax.ShapeDtypeStruct((DEPTH, N_CHUNKS, 2 * CHUNK_STATE, MXU_DIM), _BF16),
            jax.ShapeDtypeStruct((DEPTH, SSM_GROUPS, 3, SUBLANES, STATE2), _F32),
            jax.ShapeDtypeStruct((DEPTH, SSM_GROUPS, MXU_DIM, MXU_DIM), _BF16),
            jax.ShapeDtypeStruct((DEPTH, SSM_GROUPS, MXU_DIM, MXU_DIM), _BF16),
            jax.ShapeDtypeStruct((DEPTH, SSM_GROUPS, STATE2, MXU_DIM), _BF16),
        ],
        scratch_shapes=[pltpu.VMEM((MXU_DIM, GP), _F32)] * 6,
        compiler_params=pltpu.CompilerParams(vmem_limit_bytes=VMEM_LIMIT),
        name="ssm_prep",
    )(lam_re, lam_im, logdt, bt_re, bt_im, ct_re, ct_im)


def _to_time_major(x_ref, slab_ref):
    batch, steps, width = x_ref.shape
    for n in range(batch):
        for j in range(width // LANES):
            slab_ref[j, pl.ds(n, steps, stride=batch), :] = x_ref[n, :, j * LANES:(j + 1) * LANES]
    return jnp.concatenate([slab_ref[j] for j in range(width // LANES)], axis=1)


def _from_time_major(x, slab_ref, o_ref):
    batch, steps, width = o_ref.shape
    for j in range(width // LANES):
        slab_ref[j] = x[:, j * LANES:(j + 1) * LANES]
    for n in range(batch):
        for j in range(width // LANES):
            o_ref[n, :, j * LANES:(j + 1) * LANES] = slab_ref[j, pl.ds(n, steps, stride=batch), :]


BLOCKS_PER_TILE = LANES // SSM_GROUP


def _lane_block_transpose(vs):
    vs = list(vs)
    lane = lax.broadcasted_iota(jnp.int32, vs[0].shape, 1)
    block_bits = SSM_GROUP.bit_length() - 1
    for b in reversed(range(BLOCKS_PER_TILE.bit_length() - 1)):
        d = 1 << b
        hi = ((lane >> (block_bits + b)) & 1) == 1
        for r in range(BLOCKS_PER_TILE):
            if r & d:
                continue
            lo_arr, hi_arr = vs[r], vs[r + d]
            vs[r] = jnp.where(hi, pltpu.roll(hi_arr, d * SSM_GROUP, axis=1), lo_arr)
            vs[r + d] = jnp.where(hi, hi_arr, pltpu.roll(lo_arr, LANES - d * SSM_GROUP, axis=1))
    return vs


def _to_group_major(z_ref, u_ref):
    chunks = z_ref.shape[0]
    for q in range(SSM_WIDTH // LANES):
        for half in range(CHUNK_T // BLOCKS_PER_TILE):
            vs = []
            for i in range(BLOCKS_PER_TILE):
                s = half * BLOCKS_PER_TILE + i
                v = z_ref[:, s * SUBLANES:(s + 1) * SUBLANES, q * LANES:(q + 1) * LANES]
                vs.append(v.reshape(chunks * SUBLANES, LANES))
            for gl, v in enumerate(_lane_block_transpose(vs)):
                c0 = (q * BLOCKS_PER_TILE + gl) * MXU_DIM + half * LANES
                u_ref[:, c0:c0 + LANES] = v.astype(u_ref.dtype)


def _from_group_major(y_ref, o_ref):
    chunks = o_ref.shape[0]
    for q in range(SSM_WIDTH // LANES):
        for half in range(CHUNK_T // BLOCKS_PER_TILE):
            vs = []
            for gl in range(BLOCKS_PER_TILE):
                c0 = (q * BLOCKS_PER_TILE + gl) * MXU_DIM + half * LANES
                vs.append(y_ref[:, c0:c0 + LANES].astype(_F32))
            for i, v in enumerate(_lane_block_transpose(vs)):
                t = half * BLOCKS_PER_TILE + i
                o_ref[:, t * SUBLANES:(t + 1) * SUBLANES, q * LANES:(q + 1) * LANES] = (
                    v.reshape(chunks, SUBLANES, LANES))


def _cast_chunks(src_refs, dst_refs):
    for src, dst in zip(src_refs, dst_refs):
        dst[...] = src[...].astype(_BF16)


def _front_kernel(native_in, x_ref, g1_ref, win_ref, cw_ref, cb_ref,
                  ya_ref, zu_ref, u_ref, tail_ref, *rest):
    rows = ya_ref.shape[0]
    halo = (CONV_K - 1) * SUBLANES
    cw = CONV_WIDTH

    @pl.when(pl.program_id(0) == 0)
    def _():
        tail_ref[...] = jnp.zeros_like(tail_ref)

    if native_in:
        xtm_ref, slab_ref = rest
        x = _to_time_major(x_ref, slab_ref)
        xtm_ref[...] = x
    else:
        x = x_ref[...]
    h = _rmsnorm(x, g1_ref[...]).astype(_BF16)
    zu = _dot(h, win_ref[:, 3 * cw:3 * cw + SSM_WIDTH])
    zu_ref[...] = zu.reshape(zu_ref.shape)
    _to_group_major(zu_ref, u_ref)

    zb = _dot(h, win_ref[:, 0:cw])
    zc = _dot(h, win_ref[:, cw:2 * cw])
    zv = _dot(h, win_ref[:, 2 * cw:3 * cw])
    cin = zc * zv
    pad = jnp.concatenate([tail_ref[...], cin], axis=0)
    conv = cb_ref[...]
    for k in range(CONV_K):
        conv = conv + cw_ref[k:k + 1, :] * pad[k * SUBLANES:k * SUBLANES + rows]
    tail_ref[...] = cin[rows - halo:rows]
    ya_ref[...] = (zb * conv).astype(_BF16)


def _s5_chunk_kernel(u_ref, s4_ref, kin_ref, yin_ref, apat_ref, y_ref, st_ref, g4_ref, hin_ref):
    n_chunks = u_ref.shape[0] // SUBLANES
    for gi in range(S5_GROUPS_PER_STEP):
        cols = slice(gi * MXU_DIM, (gi + 1) * MXU_DIM)
        u = u_ref[:, cols]
        g4_ref[gi] = _dot(u, s4_ref[gi])
        a1, a2, a3 = apat_ref[gi, 0], apat_ref[gi, 1], apat_ref[gi, 2]
        p = jnp.zeros((SUBLANES, STATE2), _F32)
        q = jnp.zeros((SUBLANES, STATE2), _F32)
        for j in range(n_chunks):
            rs = pl.ds(j * SUBLANES, SUBLANES)
            hin_ref[gi, rs, :] = p
            p, q = (a1 * p + a2 * q + g4_ref[gi, rs, 0:STATE2],
                    a1 * q + a3 * p + g4_ref[gi, rs, STATE2:2 * STATE2])
        st_ref[gi] = p
        y_ref[:, cols] = (_dot(u, kin_ref[gi])
                          + _dot(hin_ref[gi].astype(_BF16), yin_ref[gi])).astype(y_ref.dtype)


N_BACK_WEIGHTS = 7


def _back_kernel(n_cast, x_ref, ya_ref, zu_ref, yg_ref, *refs):
    g1_ref, win_ref, d_ref, wglu_ref, bglu_ref, wbr_ref, wout_ref = refs[:N_BACK_WEIGHTS]
    refs = refs[N_BACK_WEIGHTS:]
    cast_src, xo_ref = refs[:n_cast], refs[n_cast]
    cast_dst, ys_ref = refs[n_cast + 1:2 * n_cast + 1], refs[2 * n_cast + 1]
    rows = xo_ref.shape[0]
    cw = CONV_WIDTH
    _cast_chunks(cast_src, cast_dst)

    x = x_ref[...]
    h = _rmsnorm(x, g1_ref[...]).astype(_BF16)
    ga = _dot(h, win_ref[:, 0:D_MODEL])
    gs = _dot(h, win_ref[:, D_MODEL:2 * D_MODEL])
    oa = _dot(ya_ref[...], wbr_ref[0:cw, :])

    _from_group_major(yg_ref, ys_ref)
    zu = zu_ref[...].reshape(rows, SSM_WIDTH)
    y = ys_ref[...].reshape(rows, SSM_WIDTH) + d_ref[...] * zu
    gy = _gelu_exact(y)
    ys = gy * jax.nn.sigmoid(_dot(gy.astype(_BF16), wglu_ref[...]) + bglu_ref[...])
    ob = _dot(ys.astype(_BF16), wbr_ref[cw:cw + SSM_WIDTH, :])
    m = jax.nn.sigmoid(ga) * oa + jax.nn.sigmoid(gs) * ob
    xo_ref[...] = x + _dot(m.astype(_BF16), wout_ref[...])


def _sample_layer_kernel(final_norm, x_ref, buf_ref, h0r_ref, h0i_ref, g1_ref, win_ref, cw_ref,
                         cb_ref, abar_ref, d_ref, wbu_ref, cmat_ref, wglu_ref, bglu_ref, wbr_ref,
                         wout_ref, g2_ref, wgu_ref, wdown_ref, gf_ref,
                         xo_ref, cin_ref, str_ref, sti_ref):
    cw = CONV_WIDTH
    x = x_ref[...]
    h = _rmsnorm(x, g1_ref[...]).astype(_BF16)
    zb = _dot(h, win_ref[:, 0:cw])
    zc = _dot(h, win_ref[:, cw:2 * cw])
    zv = _dot(h, win_ref[:, 2 * cw:3 * cw])
    cin = zc * zv
    cin_ref[...] = cin
    conv = cb_ref[...]
    for k in range(CONV_K - 1):
        conv = conv + cw_ref[k:k + 1, :] * buf_ref[k]
    conv = conv + cw_ref[CONV_K - 1:CONV_K, :] * cin
    ya = (zb * conv).astype(_BF16)

    zu = _dot(h, win_ref[:, 3 * cw:3 * cw + SSM_WIDTH])
    zub = zu.astype(_BF16)
    ar = abar_ref[0:1, :]
    ai = abar_ref[1:2, :]
    h0r = h0r_ref[...]
    h0i = h0i_ref[...]
    ys = []
    for k in range(N_CHUNKS):
        sl = slice(k * CHUNK_STATE, (k + 1) * CHUNK_STATE)
        bu = _dot(zub[:, k * MXU_DIM:(k + 1) * MXU_DIM], wbu_ref[k])
        hr = ar[:, sl] * h0r[:, sl] - ai[:, sl] * h0i[:, sl] + bu[:, 0:CHUNK_STATE]
        hi = ar[:, sl] * h0i[:, sl] + ai[:, sl] * h0r[:, sl] + bu[:, CHUNK_STATE:2 * CHUNK_STATE]
        str_ref[:, sl] = hr
        sti_ref[:, sl] = hi
        ys.append(_dot(hr.astype(_BF16), cmat_ref[k, 0:CHUNK_STATE, :])
                  + _dot(hi.astype(_BF16), cmat_ref[k, CHUNK_STATE:2 * CHUNK_STATE, :]))
    y = jnp.concatenate(ys, axis=1) + d_ref[...] * zu
    gy = _gelu_exact(y)
    ys = gy * jax.nn.sigmoid(_dot(gy.astype(_BF16), wglu_ref[...]) + bglu_ref[...])

    oa = _dot(ya, wbr_ref[0:cw, :])
    ob = _dot(ys.astype(_BF16), wbr_ref[cw:cw + SSM_WIDTH, :])
    g0 = 3 * cw + SSM_WIDTH
    ga = _dot(h, win_ref[:, g0:g0 + D_MODEL])
    gs = _dot(h, win_ref[:, g0 + D_MODEL:g0 + 2 * D_MODEL])
    m = jax.nn.sigmoid(ga) * oa + jax.nn.sigmoid(gs) * ob
    x = x + _dot(m.astype(_BF16), wout_ref[...])

    h2 = _rmsnorm(x, g2_ref[...]).astype(_BF16)
    acts = []
    for c0, cn in _FF_CHUNKS:
        gate = _dot(h2, wgu_ref[:, c0:c0 + cn])
        up = _dot(h2, wgu_ref[:, D_FF + c0:D_FF + c0 + cn])
        acts.append((gate * jax.nn.sigmoid(gate) * up).astype(_BF16))
    x = x + _dot(jnp.concatenate(acts, axis=1), wdown_ref[...])
    xo_ref[...] = _rmsnorm(x, gf_ref[...]) if final_norm else x


def _resident(arr, layer):
    if arr.shape[0] == DEPTH and arr.ndim >= 3:
        shape = arr.shape[1:]
        return pl.BlockSpec((None,) + shape, lambda i: (layer,) + (0,) * len(shape),
                            pipeline_mode=pl.Buffered(1))
    return pl.BlockSpec(arr.shape, lambda i: (0,) * arr.ndim, pipeline_mode=pl.Buffered(1))


def _resident_cols(arr, half):
    rows, cols = arr.shape
    return pl.BlockSpec((rows, cols // 2), lambda i: (0, half), pipeline_mode=pl.Buffered(1))


def _cast_specs(arr, layer, n_steps):
    _, rows, cols = arr.shape
    hold = 1
    while (rows * hold) % (n_steps * BF16_SUBLANES):
        hold *= 2
    chunk = rows * hold // n_steps
    in_spec = pl.BlockSpec((None, chunk, cols), lambda i: (layer, i // hold, 0))
    out_spec = pl.BlockSpec((chunk, cols), lambda i: (i // hold, 0))
    return in_spec, out_spec, jax.ShapeDtypeStruct((rows, cols), _BF16)


def _slab_scratch(rows):
    return pltpu.VMEM((D_MODEL // LANES, rows, LANES), _F32)


def _prompt_x_spec(native_in):
    if native_in:
        return pl.BlockSpec((SUBLANES, ROW_BLOCK // SUBLANES, D_MODEL), lambda i: (0, i, 0))
    return pl.BlockSpec((ROW_BLOCK, D_MODEL), lambda i: (i, 0))


_ARBITRARY = pltpu.CompilerParams(dimension_semantics=("arbitrary",), vmem_limit_bytes=VMEM_LIMIT)


def _cast_kernel(n_cast, *refs):
    _cast_chunks(refs[:n_cast], refs[n_cast:])


def _cast_weights(arrs, layer, n_steps):
    specs = [_cast_specs(a, layer, n_steps) for a in arrs]
    return pl.pallas_call(
        functools.partial(_cast_kernel, len(arrs)),
        grid=(n_steps,),
        in_specs=[s[0] for s in specs],
        out_specs=[s[1] for s in specs],
        out_shape=[s[2] for s in specs],
        compiler_params=pltpu.CompilerParams(dimension_semantics=("arbitrary",)),
        name="cast_weights",
    )(*arrs)


def _mixer_front(layer, x, g1, w_in_b, conv_w, conv_b):
    native_in = x.ndim == 3
    rows = x.shape[0] * x.shape[1] if native_in else x.shape[0]
    chunks = rows // CHUNK_ROWS
    blk_chunks = ROW_BLOCK // CHUNK_ROWS
    halo = (CONV_K - 1) * SUBLANES
    row_spec = pl.BlockSpec((ROW_BLOCK, D_MODEL), lambda i: (i, 0))
    return pl.pallas_call(
        functools.partial(_front_kernel, native_in),
        grid=(rows // ROW_BLOCK,),
        in_specs=[_prompt_x_spec(native_in), _resident(g1, layer), _resident_cols(w_in_b, 0),
                  _resident(conv_w, layer), _resident(conv_b, layer)],
        out_specs=[
            pl.BlockSpec((ROW_BLOCK, CONV_WIDTH), lambda i: (i, 0)),
            pl.BlockSpec((blk_chunks, CHUNK_ROWS, SSM_WIDTH), lambda i: (i, 0, 0)),
            pl.BlockSpec((blk_chunks * SUBLANES, GROUP_LANES), lambda i: (i, 0)),
            pl.BlockSpec((halo, CONV_WIDTH), lambda i: (0, 0)),
        ] + ([row_spec] if native_in else []),
        out_shape=[
            jax.ShapeDtypeStruct((rows, CONV_WIDTH), _BF16),
            jax.ShapeDtypeStruct((chunks, CHUNK_ROWS, SSM_WIDTH), _F32),
            jax.ShapeDtypeStruct((chunks * SUBLANES, GROUP_LANES), _BF16),
            jax.ShapeDtypeStruct((halo, CONV_WIDTH), _F32),
        ] + ([jax.ShapeDtypeStruct((rows, D_MODEL), _F32)] if native_in else []),
        scratch_shapes=[_slab_scratch(ROW_BLOCK)] if native_in else [],
        compiler_params=_ARBITRARY,
        name="mixer_front",
    )(x, g1, w_in_b, conv_w, conv_b)


def _s5_chunked(layer, u, s4, kin, yin, apat):
    rows = u.shape[0]
    gs = S5_GROUPS_PER_STEP
    wide = pl.BlockSpec((rows, gs * MXU_DIM), lambda i: (0, i))
    table = lambda r, c: pl.BlockSpec((None, gs, r, c), lambda i: (layer, i, 0, 0))
    return pl.pallas_call(
        _s5_chunk_kernel,
        grid=(SSM_GROUPS // gs,),
        in_specs=[wide, table(MXU_DIM, MXU_DIM), table(MXU_DIM, MXU_DIM), table(STATE2, MXU_DIM),
                  pl.BlockSpec((None, gs, 3, SUBLANES, STATE2), lambda i: (layer, i, 0, 0, 0))],
        out_specs=[wide, pl.BlockSpec((gs, SUBLANES, STATE2), lambda i: (i, 0, 0))],
        out_shape=[jax.ShapeDtypeStruct((rows, GROUP_LANES), _BF16),
                   jax.ShapeDtypeStruct((SSM_GROUPS, SUBLANES, STATE2), _F32)],
        scratch_shapes=[pltpu.VMEM((gs, rows, MXU_DIM), _F32), pltpu.VMEM((gs, rows, STATE2), _F32)],
        compiler_params=_ARBITRARY,
        name="s5_chunked",
    )(u, s4, kin, yin, apat)


def _mixer_back(layer, x, ya, zu, yg, weights, w_in_b, cast):
    rows = x.shape[0]
    n_steps = rows // ROW_BLOCK
    blk_chunks = ROW_BLOCK // CHUNK_ROWS
    g1, d_skip, w_glu_b, bg, w_branch_b, w_out_b = weights
    row_spec = pl.BlockSpec((ROW_BLOCK, D_MODEL), lambda i: (i, 0))
    cast_specs = [_cast_specs(a, layer, n_steps) for a in cast]
    return pl.pallas_call(
        functools.partial(_back_kernel, len(cast)),
        grid=(n_steps,),
        in_specs=([row_spec,
                   pl.BlockSpec((ROW_BLOCK, CONV_WIDTH), lambda i: (i, 0)),
                   pl.BlockSpec((blk_chunks, CHUNK_ROWS, SSM_WIDTH), lambda i: (i, 0, 0)),
                   pl.BlockSpec((blk_chunks * SUBLANES, GROUP_LANES), lambda i: (i, 0)),
                   _resident(g1, layer), _resident_cols(w_in_b, 1), _resident(d_skip, layer),
                   _resident(w_glu_b, layer), _resident(bg, layer), _resident(w_branch_b, layer),
                   _resident(w_out_b, layer)]
                  + [s[0] for s in cast_specs]),
        out_specs=[row_spec] + [s[1] for s in cast_specs],
        out_shape=[jax.ShapeDtypeStruct((rows, D_MODEL), _F32)] + [s[2] for s in cast_specs],
        scratch_shapes=[pltpu.VMEM((blk_chunks, CHUNK_ROWS, SSM_WIDTH), _F32)],
        compiler_params=_ARBITRARY,
        name="mixer_back",
    )(x, ya, zu, yg, g1, w_in_b, d_skip, w_glu_b, bg, w_branch_b, w_out_b, *cast)


def _sample_layer(layer, x, buf, h0r, h0i, weights, g2, wgu, wdown, gf, final_norm):
    rows = x.shape[0]
    full = lambda shape: pl.BlockSpec(shape, lambda i: (0,) * len(shape))
    state = lambda shape: pl.BlockSpec((None,) + shape, lambda i: (layer,) + (0,) * len(shape))
    weights = tuple(weights) + (g2, wgu, wdown)
    return pl.pallas_call(
        functools.partial(_sample_layer_kernel, final_norm),
        grid=(1,),
        in_specs=([full((rows, D_MODEL)), state((CONV_K - 1, rows, CONV_WIDTH)),
                   state((rows, GP)), state((rows, GP))] + [_resident(w, layer) for w in weights]
                  + [_resident(gf, 0)]),
        out_specs=[full((rows, D_MODEL)), full((rows, CONV_WIDTH)),
                   full((rows, GP)), full((rows, GP))],
        out_shape=[
            jax.ShapeDtypeStruct((rows, D_MODEL), _F32),
            jax.ShapeDtypeStruct((rows, CONV_WIDTH), _F32),
            jax.ShapeDtypeStruct((rows, GP), _F32),
            jax.ShapeDtypeStruct((rows, GP), _F32),
        ],
        compiler_params=_ARBITRARY,
        name="sample_layer",
    )(x, buf, h0r, h0i, *weights, gf)


_FF_CHUNKS = ((0, 1024), (1024, 1024), (2048, 768))


def _ffn_kernel(final_norm, native_out, n_cast, x_ref, g2_ref, wgu_ref, wdown_ref, gf_ref, *refs):
    cast_src, xo_ref = refs[:n_cast], refs[n_cast]
    cast_dst, (act_ref, *slab_ref) = refs[n_cast + 1:2 * n_cast + 1], refs[2 * n_cast + 1:]
    _cast_chunks(cast_src, cast_dst)
    x = x_ref[...]
    h2 = _rmsnorm(x, g2_ref[...]).astype(_BF16)
    for c0, cn in _FF_CHUNKS:
        gate = _dot(h2, wgu_ref[:, c0:c0 + cn])
        up = _dot(h2, wgu_ref[:, D_FF + c0:D_FF + c0 + cn])
        act_ref[:, c0:c0 + cn] = (gate * jax.nn.sigmoid(gate) * up).astype(_BF16)
    xn = x + _dot(act_ref[...], wdown_ref[...])
    if final_norm:
        xn = _rmsnorm(xn, gf_ref[...])
    if native_out:
        _from_time_major(xn, slab_ref[0], xo_ref)
    else:
        xo_ref[...] = xn


def _ffn(layer, x, g2, wgu, wdown, gf, final_norm, native_out, cast=()):
    rows = x.shape[0]
    rb = min(FFN_ROW_BLOCK, rows)
    cast_specs = [_cast_specs(a, layer + 1, rows // rb) for a in cast]
    row_spec = pl.BlockSpec((rb, D_MODEL), lambda i: (i, 0))
    if native_out:
        out_spec = pl.BlockSpec((SUBLANES, rb // SUBLANES, D_MODEL), lambda i: (0, i, 0))
        out_shape = jax.ShapeDtypeStruct((SUBLANES, rows // SUBLANES, D_MODEL), _F32)
    else:
        out_spec = row_spec
        out_shape = jax.ShapeDtypeStruct((rows, D_MODEL), _F32)
    outs = pl.pallas_call(
        functools.partial(_ffn_kernel, final_norm, native_out, len(cast)),
        grid=(rows // rb,),
        in_specs=([row_spec, _resident(g2, layer), _resident(wgu, layer), _resident(wdown, layer),
                   _resident(gf, 0)] + [s[0] for s in cast_specs]),
        out_specs=[out_spec] + [s[1] for s in cast_specs],
        out_shape=[out_shape] + [s[2] for s in cast_specs],
        scratch_shapes=[pltpu.VMEM((rb, D_FF), _BF16)] + ([_slab_scratch(rb)] if native_out else []),
        compiler_params=_ARBITRARY,
        name="ffn",
    )(x, g2, wgu, wdown, gf, *cast)
    return outs if cast else outs[0]


def kernel(x_prompt, x_sample, state_conv, state_ssm_re, state_ssm_im, norm1_g, w_in, conv_w,
           conv_b, ssm_lam_re, ssm_lam_im, ssm_log_dt, ssm_b_re, ssm_b_im, ssm_c_re, ssm_c_im,
           ssm_d, w_glu, b_glu, w_branch, w_out, norm2_g, w_gate_up, w_down, final_g):
    batch, seq, _ = x_prompt.shape
    dec_batch = x_sample.shape[0]
    assert batch == SUBLANES and x_sample.shape[1] == 1

    abar, wbu, cmat, apat, s4, kin, yin = _ssm_prep(
        ssm_lam_re, ssm_lam_im, ssm_log_dt, ssm_b_re, ssm_b_im, ssm_c_re, ssm_c_im)
    g1 = norm1_g.reshape(DEPTH, 1, D_MODEL)
    g2 = norm2_g.reshape(DEPTH, 1, D_MODEL)
    gf = final_g.reshape(1, D_MODEL)
    cb = conv_b.reshape(DEPTH, 1, CONV_WIDTH)
    d_skip = ssm_d.reshape(DEPTH, 1, SSM_WIDTH)
    bg = b_glu.reshape(DEPTH, 1, SSM_WIDTH)
    mixer_f32 = (w_in, w_glu, w_branch, w_out)
    ffn_f32 = (w_gate_up, w_down)
    mixer_b = _cast_weights(mixer_f32, 0, CAST_STEPS)

    xp = x_prompt
    xs = x_sample.reshape(dec_batch, D_MODEL)
    bufs = state_conv.transpose(0, 2, 1, 3)
    h0r = state_ssm_re.reshape(DEPTH, dec_batch, GP)
    h0i = state_ssm_im.reshape(DEPTH, dec_batch, GP)

    p_conv, p_st, s_cin, s_re, s_im = [], [], [], [], []
    for l in range(DEPTH):
        last = l == DEPTH - 1
        w_in_b, w_glu_b, w_branch_b, w_out_b = mixer_b

        ya, zu, u, tail, *x_tm = _mixer_front(l, xp, g1, w_in_b, conv_w, cb)
        if x_tm:
            xp, = x_tm
        yg, st = _s5_chunked(l, u, s4, kin, yin, apat)
        xp, w_gate_up_b, w_down_b = _mixer_back(
            l, xp, ya, zu, yg, (g1, d_skip, w_glu_b, bg, w_branch_b, w_out_b), w_in_b, ffn_f32)
        if last:
            xp = _ffn(l, xp, g2, w_gate_up_b, w_down_b, gf, True, native_out=True)
        else:
            xp, *mixer_b = _ffn(l, xp, g2, w_gate_up_b, w_down_b, gf, False, native_out=False,
                                cast=mixer_f32)
        p_conv.append(tail)
        p_st.append(st)

        sample_weights = (g1, w_in_b, conv_w, cb, abar, d_skip, wbu, cmat, w_glu_b, bg,
                          w_branch_b, w_out_b)
        xs, cin, ssr, ssi = _sample_layer(l, xs, bufs, h0r, h0i, sample_weights,
                                          g2, w_gate_up_b, w_down_b, gf, last)
        s_cin.append(cin)
        s_re.append(ssr)
        s_im.append(ssi)

    y_prompt = xp
    y_sample = xs.reshape(dec_batch, 1, D_MODEL)
    prompt_conv = (jnp.stack(p_conv).reshape(DEPTH, CONV_K - 1, batch, CONV_WIDTH)
                   .transpose(0, 2, 1, 3))
    st_all = jnp.stack(p_st).transpose(0, 2, 1, 3)
    sample_conv = jnp.concatenate([state_conv[:, :, 1:], jnp.stack(s_cin)[:, :, None, :]], axis=2)
    group_shape = (DEPTH, dec_batch, SSM_GROUPS, SSM_STATE)
    return (y_prompt, y_sample, prompt_conv, st_all[..., :SSM_STATE], st_all[..., SSM_STATE:],
            sample_conv, jnp.stack(s_re).reshape(group_shape), jnp.stack(s_im).reshape(group_shape))
```
